```python
import math
import jax, jax.numpy as jnp
from jax import lax
import numpy as np

D_MODEL = 1024
BATCH = 2
SEQ = 16384
DEPTH = 4
DEC_BATCH = 8
DEC_SEQ = 64
PAST_LEN = 4096

CHUNK = 64
EPS = 1e-6
ATT_HEADS = 8
ATT_HEAD_DIM = 64
ATT_W = ATT_HEADS * ATT_HEAD_DIM
ATT_BACK_CHUNKS = 8
ATT_PAST = ATT_BACK_CHUNKS * CHUNK
REL_CLIP = 128
SSD_HEADS = 16
SSD_HEAD_DIM = 64
SSD_W = SSD_HEADS * SSD_HEAD_DIM
SSD_GROUPS = 2
SSD_STATE = 128
SSD_HPG = SSD_HEADS // SSD_GROUPS
CONV_K = 4
CONV_DIM = SSD_W + 2 * SSD_GROUPS * SSD_STATE
MEM_LEN = 256
MEM_HEADS = 4
MEM_HEAD_DIM = 128
MEM_W = MEM_HEADS * MEM_HEAD_DIM

D_MIX = ATT_W + SSD_W + MEM_W
IN_DIM = 4 * ATT_W + SSD_W + CONV_DIM + SSD_HEADS + 2 * MEM_W
IN_SPLITS = (ATT_W, 2 * ATT_W, 3 * ATT_W, 4 * ATT_W,
             4 * ATT_W + SSD_W,
             4 * ATT_W + SSD_W + CONV_DIM,
             4 * ATT_W + SSD_W + CONV_DIM + SSD_HEADS,
             4 * ATT_W + SSD_W + CONV_DIM + SSD_HEADS + MEM_W)

kernel_name = "hybrid_chunk_attn_ssd_memory_stream_step"


def rms_norm(x, g):
    xf = x.astype(jnp.float32)
    y = xf * lax.rsqrt(jnp.mean(xf * xf, axis=-1, keepdims=True) + EPS)
    return (y * g.astype(jnp.float32)).astype(x.dtype)


def rel_bias_lookup(rel_bias, dist):
    return rel_bias[:, jnp.clip(dist, -REL_CLIP, REL_CLIP) + REL_CLIP].astype(jnp.float32)


def band_attention_prompt(q, k, v, rel_bias):
    Bt, L, H, Dh = q.shape
    nc = L // CHUNK
    band = ATT_PAST + CHUNK
    kp = jnp.pad(k, ((0, 0), (ATT_PAST, 0), (0, 0), (0, 0)))
    vp = jnp.pad(v, ((0, 0), (ATT_PAST, 0), (0, 0), (0, 0)))
    kj = jnp.arange(band)
    dist = jnp.arange(CHUNK)[:, None] + ATT_PAST - kj[None, :]
    bias = rel_bias_lookup(rel_bias, dist)
    scale = Dh ** -0.5

    def one_chunk(c):
        start = c * CHUNK
        qc = lax.dynamic_slice_in_dim(q, start, CHUNK, axis=1)
        kc = lax.dynamic_slice_in_dim(kp, start, band, axis=1)
        vc = lax.dynamic_slice_in_dim(vp, start, band, axis=1)
        valid = (kj + start - ATT_PAST) >= 0
        s = jnp.einsum('bqhd,bkhd->bhqk', qc, kc, preferred_element_type=jnp.float32) * scale + bias
        s = jnp.where(valid, s, -jnp.inf)
        p = jax.nn.softmax(s, axis=-1)
        return jnp.einsum('bhqk,bkhd->bqhd', p.astype(vc.dtype), vc)

    out = lax.map(one_chunk, jnp.arange(nc))
    return out.transpose(1, 0, 2, 3, 4).reshape(Bt, L, H, Dh)


def band_attention_sample(q, k_new, v_new, k_past, v_past, rel_bias):
    T = q.shape[1]
    P = k_past.shape[1]
    k = jnp.concatenate([k_past, k_new], axis=1)
    v = jnp.concatenate([v_past, v_new], axis=1)
    dist = jnp.arange(T)[:, None] + P - jnp.arange(P + T)[None, :]
    bias = rel_bias_lookup(rel_bias, dist)
    s = jnp.einsum('bqhd,bkhd->bhqk', q, k, preferred_element_type=jnp.float32) * (q.shape[-1] ** -0.5) + bias
    p = jax.nn.softmax(s, axis=-1)
    return jnp.einsum('bhqk,bkhd->bqhd', p.astype(v.dtype), v)


def memory_attention(q, mk, mv):
    s = jnp.einsum('blhd,bmhd->bhlm', q, mk, preferred_element_type=jnp.float32) * (q.shape[-1] ** -0.5)
    p = jax.nn.softmax(s, axis=-1)
    return jnp.einsum('bhlm,bmhd->blhd', p.astype(mv.dtype), mv)


def causal_conv(xbc, conv_state, w, b):
    T = xbc.shape[1]
    xp = jnp.concatenate([conv_state.astype(xbc.dtype), xbc], axis=1)
    y = b + xp[:, 0:T] * w[0]
    for tap in range(1, CONV_K):
        y = y + xp[:, tap:tap + T] * w[tap]
    return jax.nn.silu(y), xp[:, -(CONV_K - 1):]


def ssd_scan(x, dt, A, Bm, Cm, h0, chunk):
    Bt, L = x.shape[:2]
    nc = L // chunk

    def to_chunks(t):
        return t.reshape((Bt, nc, chunk) + t.shape[2:]).swapaxes(0, 1)

    mask = jnp.tril(jnp.ones((chunk, chunk), dtype=bool))[None, :, :, None, None]

    def step(h, inp):
        xc, dtc, bc, cc = inp
        xf = xc.astype(jnp.float32)
        bf = bc.astype(jnp.float32)
        cf = cc.astype(jnp.float32)
        acum = jnp.cumsum(dtc * A, axis=1)
        diff = acum[:, :, None] - acum[:, None, :]
        decay = jnp.exp(jnp.where(mask, diff, -jnp.inf))
        cb = jnp.einsum('blgn,bsgn->blsg', cf, bf)
        w = cb[..., None] * decay * dtc[:, None]
        y = jnp.einsum('blsge,bsgep->blgep', w, xf)
        y = y + jnp.einsum('blgn,bgepn->blgep', cf, h) * jnp.exp(acum)[..., None]
        to_end = jnp.exp(acum[:, -1:] - acum) * dtc
        h_new = h * jnp.exp(acum[:, -1])[..., None, None] + jnp.einsum('blgn,blge,blgep->bgepn', bf, to_end, xf)
        return h_new, y.astype(x.dtype)

    hT, ys = lax.scan(step, h0, (to_chunks(x), to_chunks(dt), to_chunks(Bm), to_chunks(Cm)))
    return ys.swapaxes(0, 1).reshape(x.shape), hT


def trunk_layer(x, mem_k, mem_v, conv_state, ssd_h0, att_k_past, att_v_past,
                w_in, w_out, g_pre, g_post, rel_bias, conv_w, conv_b, dt_bias, a_log, d_skip, g_ssd,
                ssd_chunk):
    Bt, T, _ = x.shape
    h = rms_norm(x, g_pre)
    proj = h @ w_in
    aq, ak, av, ag, sz, sxbc, sdt, mq, mg = jnp.split(proj, IN_SPLITS, axis=-1)

    aq = aq.reshape(Bt, T, ATT_HEADS, ATT_HEAD_DIM)
    ak = ak.reshape(Bt, T, ATT_HEADS, ATT_HEAD_DIM)
    av = av.reshape(Bt, T, ATT_HEADS, ATT_HEAD_DIM)
    if att_k_past is None:
        a_out = band_attention_prompt(aq, ak, av, rel_bias)
        keep = min(ATT_PAST, T)
        k_rows, v_rows = ak[:, T - keep:], av[:, T - keep:]
    else:
        a_out = band_attention_sample(aq, ak, av, att_k_past.astype(x.dtype), att_v_past.astype(x.dtype), rel_bias)
        k_rows, v_rows = ak, av
    a_out = a_out.reshape(Bt, T, ATT_W) * jax.nn.silu(ag)

    xbc, new_conv = causal_conv(sxbc, conv_state, conv_w, conv_b)
    xs, Bm, Cm = jnp.split(xbc, (SSD_W, SSD_W + SSD_GROUPS * SSD_STATE), axis=-1)
    xs = xs.reshape(Bt, T, SSD_GROUPS, SSD_HPG, SSD_HEAD_DIM)
    Bm = Bm.reshape(Bt, T, SSD_GROUPS, SSD_STATE)
    Cm = Cm.reshape(Bt, T, SSD_GROUPS, SSD_STATE)
    dt = jax.nn.softplus(sdt.astype(jnp.float32) + dt_bias.astype(jnp.float32)).reshape(Bt, T, SSD_GROUPS, SSD_HPG)
    A = -jnp.exp(a_log.astype(jnp.float32)).reshape(SSD_GROUPS, SSD_HPG)
    h0 = ssd_h0.astype(jnp.float32).reshape(Bt, SSD_GROUPS, SSD_HPG, SSD_HEAD_DIM, SSD_STATE)
    ys, hT = ssd_scan(xs, dt, A, Bm, Cm, h0, ssd_chunk)
    ys = ys + xs * d_skip.reshape(SSD_GROUPS, SSD_HPG)[..., None].astype(xs.dtype)
    ys = ys.reshape(Bt, T, SSD_W) * jax.nn.silu(sz)
    yg = ys.reshape(Bt, T, SSD_GROUPS, SSD_W // SSD_GROUPS).astype(jnp.float32)
    yg = yg * lax.rsqrt(jnp.mean(yg * yg, axis=-1, keepdims=True) + EPS)
    s_out = (yg.reshape(Bt, T, SSD_W) * g_ssd.astype(jnp.float32)).astype(x.dtype)
    new_ssd = hT.reshape(Bt, SSD_HEADS, SSD_HEAD_DIM, SSD_STATE).astype(x.dtype)

    m_out = memory_attention(mq.reshape(Bt, T, MEM_HEADS, MEM_HEAD_DIM), mem_k.astype(x.dtype), mem_v.astype(x.dtype))
    m_out = m_out.reshape(Bt, T, MEM_W) * jax.nn.silu(mg)

    mix = jnp.concatenate([a_out, s_out, m_out], axis=-1) @ w_out
    y = x + rms_norm(mix, g_post)
    return y, k_rows, v_rows, new_conv, new_ssd


def setup_inputs(seed: int = 0) -> dict:
    key = jax.random.key(seed)
    ks = jax.random.split(key, 24)
    f32 = jnp.float32

    def nrm(k, shape, s):
        return jax.random.normal(k, shape, f32) * s

    att_rows = min(ATT_PAST, PAST_LEN)
    dt0 = jnp.exp(jax.random.uniform(ks[18], (DEPTH, SSD_HEADS), f32, math.log(1e-3), math.log(1e-1)))
    return {
        "x_prompt": nrm(ks[0], (BATCH, SEQ, D_MODEL), 1.0),
        "x_sample": nrm(ks[1], (DEC_BATCH, DEC_SEQ, D_MODEL), 1.0),
        "cache_att_k": nrm(ks[2], (DEPTH, DEC_BATCH, att_rows, ATT_HEADS, ATT_HEAD_DIM), 1.0),
        "cache_att_v": nrm(ks[3], (DEPTH, DEC_BATCH, att_rows, ATT_HEADS, ATT_HEAD_DIM), 1.0),
        "state_conv": nrm(ks[4], (DEPTH, DEC_BATCH, CONV_K - 1, CONV_DIM), 1.0),
        "state_ssd": nrm(ks[5], (DEPTH, DEC_BATCH, SSD_HEADS, SSD_HEAD_DIM, SSD_STATE), 0.1),
        "cache_mem_k": nrm(ks[6], (DEPTH, DEC_BATCH, MEM_LEN, MEM_HEADS, MEM_HEAD_DIM), 1.0),
        "cache_mem_v": nrm(ks[7], (DEPTH, DEC_BATCH, MEM_LEN, MEM_HEADS, MEM_HEAD_DIM), 1.0),
        "mem_prompt": nrm(ks[8], (BATCH, MEM_LEN, D_MODEL), 1.0),
        "w_in": nrm(ks[9], (DEPTH, D_MODEL, IN_DIM), D_MODEL ** -0.5),
        "w_out": nrm(ks[10], (DEPTH, D_MIX, D_MODEL), D_MIX ** -0.5),
        "w_mem_kv": nrm(ks[11], (DEPTH, D_MODEL, 2 * MEM_W), D_MODEL ** -0.5),
        "norm_pre": 1.0 + nrm(ks[12], (DEPTH, D_MODEL), 0.02),
        "norm_post": 1.0 + nrm(ks[13], (DEPTH, D_MODEL), 0.02),
        "norm_mem": 1.0 + nrm(ks[14], (DEPTH, D_MODEL), 0.02),
        "rel_bias": nrm(ks[15], (DEPTH, ATT_HEADS, 2 * REL_CLIP + 1), 0.1),
        "conv_w": nrm(ks[16], (DEPTH, CONV_K, CONV_DIM), CONV_K ** -0.5),
        "conv_b": nrm(ks[17], (DEPTH, CONV_DIM), 0.02),
        "dt_bias": dt0 + jnp.log(-jnp.expm1(-dt0)),
        "a_log": jnp.log(jax.random.uniform(ks[19], (DEPTH, SSD_HEADS), f32, 1.0, 16.0)),
        "d_skip": 1.0 + nrm(ks[20], (DEPTH, SSD_HEADS), 0.1),
        "norm_ssd": 1.0 + nrm(ks[21], (DEPTH, SSD_W), 0.02),
    }


def reference(x_prompt, x_sample, cache_att_k, cache_att_v, state_conv, state_ssd, cache_mem_k, cache_mem_v,
              mem_prompt, w_in, w_out, w_mem_kv, norm_pre, norm_post, norm_mem, rel_bias, conv_w, conv_b,
              dt_bias, a_log, d_skip, norm_ssd):
    Bp = x_prompt.shape[0]
    hp, hs = x_prompt, x_sample
    p_k, p_v, p_c, p_s, p_mk, p_mv = [], [], [], [], [], []
    s_k, s_v, s_c, s_s = [], [], [], []
    for l in range(DEPTH):
        lw = (w_in[l], w_out[l], norm_pre[l], norm_post[l], rel_bias[l], conv_w[l], conv_b[l],
              dt_bias[l], a_log[l], d_skip[l], norm_ssd[l])
        mkv = rms_norm(mem_prompt, norm_mem[l]) @ w_mem_kv[l]
        mk = mkv[..., :MEM_W].reshape(Bp, MEM_LEN, MEM_HEADS, MEM_HEAD_DIM)
        mv = mkv[..., MEM_W:].reshape(Bp, MEM_LEN, MEM_HEADS, MEM_HEAD_DIM)
        conv0 = jnp.zeros((Bp, CONV_K - 1, CONV_DIM), x_prompt.dtype)
        ssd0 = jnp.zeros((Bp, SSD_HEADS, SSD_HEAD_DIM, SSD_STATE), jnp.float32)
        hp, ka, va, cs, ss = trunk_layer(hp, mk, mv, conv0, ssd0, None, None, *lw, CHUNK)
        p_k.append(ka); p_v.append(va); p_c.append(cs); p_s.append(ss); p_mk.append(mk); p_mv.append(mv)
        hs, kb, vb, cb, sb = trunk_layer(hs, cache_mem_k[l], cache_mem_v[l], state_conv[l], state_ssd[l],
                                         cache_att_k[l], cache_att_v[l], *lw, x_sample.shape[1])
        s_k.append(kb); s_v.append(vb); s_c.append(cb); s_s.append(sb)
    p_att_k = jnp.stack(p_k)
    p_att_v = jnp.stack(p_v)
    p_conv = jnp.stack(p_c)
    p_ssd = jnp.stack(p_s)
    p_mem_k = jnp.stack(p_mk)
    p_mem_v = jnp.stack(p_mv)
    s_att_k = jnp.stack(s_k)
    s_att_v = jnp.stack(s_v)
    s_conv = jnp.stack(s_c)
    s_ssd = jnp.stack(s_s)
    return (hp, hs, p_att_k, p_att_v, p_conv, p_ssd, p_mem_k, p_mem_v, s_att_k, s_att_v, s_conv, s_ssd)
```

```python
import functools

import jax
import jax.numpy as jnp
from jax import lax
from jax.experimental import pallas as pl
from jax.experimental.pallas import tpu as pltpu

F32 = jnp.float32
BF16 = jnp.bfloat16
NEG_INF = float("-inf")

D_MODEL = 1024
DEPTH = 4
CHUNK = 64
EPS = 1e-6
ATT_HEADS = 8
ATT_HEAD_DIM = 64
ATT_W = ATT_HEADS * ATT_HEAD_DIM
ATT_PAST = 8 * CHUNK
BAND = ATT_PAST + CHUNK
REL_CLIP = 128
SSD_HEADS = 16
SSD_HEAD_DIM = 64
SSD_W = SSD_HEADS * SSD_HEAD_DIM
SSD_GROUPS = 2
SSD_STATE = 128
CONV_K = 4
CONV_DIM = SSD_W + 2 * SSD_GROUPS * SSD_STATE
MEM_LEN = 256
MEM_HEADS = 4
MEM_HEAD_DIM = 128
MEM_W = MEM_HEADS * MEM_HEAD_DIM
D_MIX = ATT_W + SSD_W + MEM_W
ATT_SCALE = ATT_HEAD_DIM ** -0.5
MEM_SCALE = MEM_HEAD_DIM ** -0.5

LANES = 128
SUBLANES = 8
VMEM_LIMIT = 56 * 1024 * 1024

C_Q, C_K, C_V, C_AG = 0, ATT_W, 2 * ATT_W, 3 * ATT_W
C_Z = 4 * ATT_W
C_XBC = C_Z + SSD_W
C_MQ = C_XBC + CONV_DIM
C_MG = C_MQ + MEM_W
C_END = C_MG + MEM_W

NT_DIMS = (((1,), (1,)), ((), ()))
TN_DIMS = (((0,), (0,)), ((), ()))


def _silu(y):
    return y * (1.0 / (1.0 + jnp.exp(-y)))


def _rms_scale(x):
    return lax.rsqrt(jnp.mean(x * x, axis=-1, keepdims=True) + EPS)


def _memkv_kernel(x_ref, g_ref, w_ref, o32_ref, o16_ref):
    x = x_ref[0]
    h = ((x * _rms_scale(x)) * g_ref[0]).astype(BF16)
    kv = jnp.dot(h, w_ref[0], preferred_element_type=F32)
    o32_ref[0, 0] = kv
    o16_ref[0, 0] = kv.astype(BF16)


def _memkv(mem_prompt, norm_mem, w_mem_kv_bf):
    bp = mem_prompt.shape[0]
    return pl.pallas_call(
        _memkv_kernel,
        grid=(DEPTH, bp),
        in_specs=[
            pl.BlockSpec((1, MEM_LEN, D_MODEL), lambda l, b: (b, 0, 0)),
            pl.BlockSpec((1, 1, D_MODEL), lambda l, b: (l, 0, 0)),
            pl.BlockSpec((1, D_MODEL, 2 * MEM_W), lambda l, b: (l, 0, 0)),
        ],
        out_specs=[
            pl.BlockSpec((1, 1, MEM_LEN, 2 * MEM_W), lambda l, b: (l, b, 0, 0)),
            pl.BlockSpec((1, 1, MEM_LEN, 2 * MEM_W), lambda l, b: (l, b, 0, 0)),
        ],
        out_shape=[
            jax.ShapeDtypeStruct((DEPTH, bp, MEM_LEN, 2 * MEM_W), F32),
            jax.ShapeDtypeStruct((DEPTH, bp, MEM_LEN, 2 * MEM_W), BF16),
        ],
        compiler_params=pltpu.CompilerParams(dimension_semantics=("arbitrary", "arbitrary")),
        name="memkv",
    )(mem_prompt, norm_mem.reshape(DEPTH, 1, D_MODEL), w_mem_kv_bf)


def _inproj_kernel(x_ref, g_ref, w_ref, wdt_ref, cs_ref, cw_ref, cb_ref, dtb_ref,
                   q_ref, k_ref, v_ref, ag_ref, z_ref, xs_ref, bc_ref, dt_ref, mq_ref, mg_ref,
                   krow_ref, vrow_ref, crow_ref, ext_ref, *, tm, nt, keep_tiles):
    i = pl.program_id(1)
    x = x_ref[0]
    h = ((x * _rms_scale(x)) * g_ref[...]).astype(BF16)

    def proj(lo, hi):
        return jnp.dot(h, w_ref[:, lo:hi], preferred_element_type=F32)

    q_ref[0] = (proj(C_Q, C_K) * ATT_SCALE).astype(BF16)
    kf = proj(C_K, C_V)
    vf = proj(C_V, C_AG)
    k_ref[0] = kf.astype(BF16)
    v_ref[0] = vf.astype(BF16)

    @pl.when(i >= nt - keep_tiles)
    def _():
        off = pl.multiple_of((i - (nt - keep_tiles)) * tm, tm)
        krow_ref[0, pl.ds(off, tm), :] = kf
        vrow_ref[0, pl.ds(off, tm), :] = vf

    ag_ref[0] = _silu(proj(C_AG, C_Z)).astype(BF16)
    z_ref[0] = _silu(proj(C_Z, C_XBC)).astype(BF16)
    mq_ref[0] = (proj(C_MQ, C_MG) * MEM_SCALE).astype(BF16)
    mg_ref[0] = _silu(proj(C_MG, C_END)).astype(BF16)

    dtr = jnp.dot(h, wdt_ref[...], preferred_element_type=F32) + dtb_ref[...]
    dt_ref[0] = jnp.maximum(dtr, 0.0) + jnp.log1p(jnp.exp(-jnp.abs(dtr)))

    xbc = proj(C_XBC, C_MQ)

    @pl.when(i == 0)
    def _():
        ext_ref[0:SUBLANES, :] = cs_ref[0]

    ext_ref[SUBLANES:SUBLANES + tm, :] = xbc
    y = cb_ref[...] + xbc * cw_ref[CONV_K - 1:CONV_K, :]
    for tap in range(CONV_K - 1):
        lo = SUBLANES - (CONV_K - 1) + tap
        y = y + ext_ref[lo:lo + tm, :] * cw_ref[tap:tap + 1, :]
    tail = xbc[tm - SUBLANES:tm, :]
    ext_ref[0:SUBLANES, :] = tail
    crow_ref[0] = tail
    xc = _silu(y)
    xs_ref[0] = xc[:, 0:SSD_W].astype(BF16)
    bc_ref[0] = xc[:, SSD_W:CONV_DIM].astype(BF16)


def _inproj(x, conv_state8, lw, *, tm, keep):
    bt, t, _ = x.shape
    nt = t // tm
    keep_tiles = keep // tm
    tile = lambda w: pl.BlockSpec((1, tm, w), lambda b, i: (b, i, 0))
    const2 = lambda a: pl.BlockSpec(a.shape, lambda b, i: (0, 0))
    bf = lambda w: jax.ShapeDtypeStruct((bt, t, w), BF16)
    bc_w = CONV_DIM - SSD_W
    kernel = functools.partial(_inproj_kernel, tm=tm, nt=nt, keep_tiles=keep_tiles)
    return pl.pallas_call(
        kernel,
        grid=(bt, nt),
        in_specs=[
            tile(D_MODEL),
            const2(lw["g_pre"]), const2(lw["w_main"]), const2(lw["w_dt"]),
            pl.BlockSpec((1, SUBLANES, CONV_DIM), lambda b, i: (b, 0, 0)),
            const2(lw["conv_w"]), const2(lw["conv_b"]), const2(lw["dt_bias"]),
        ],
        out_specs=[
            tile(ATT_W), tile(ATT_W), tile(ATT_W), tile(ATT_W), tile(SSD_W), tile(SSD_W), tile(bc_w),
            tile(LANES), tile(MEM_W), tile(MEM_W),
            pl.BlockSpec((1, keep, ATT_W), lambda b, i: (b, 0, 0)),
            pl.BlockSpec((1, keep, ATT_W), lambda b, i: (b, 0, 0)),
            pl.BlockSpec((1, SUBLANES, CONV_DIM), lambda b, i: (b, 0, 0)),
        ],
        out_shape=[
            bf(ATT_W), bf(ATT_W), bf(ATT_W), bf(ATT_W), bf(SSD_W), bf(SSD_W), bf(bc_w),
            jax.ShapeDtypeStruct((bt, t, LANES), F32), bf(MEM_W), bf(MEM_W),
            jax.ShapeDtypeStruct((bt, keep, ATT_W), F32),
            jax.ShapeDtypeStruct((bt, keep, ATT_W), F32),
            jax.ShapeDtypeStruct((bt, SUBLANES, CONV_DIM), F32),
        ],
        scratch_shapes=[pltpu.VMEM((SUBLANES + tm, CONV_DIM), F32)],
        compiler_params=pltpu.CompilerParams(
            dimension_semantics=("arbitrary", "arbitrary"), vmem_limit_bytes=VMEM_LIMIT),
        name="inproj",
    )(x, lw["g_pre"], lw["w_main"], lw["w_dt"], conv_state8, lw["conv_w"], lw["conv_b"], lw["dt_bias"])


def _ssd_kernel(xs_ref, bc_ref, dt_ref, z_ref, h0_ref, alog_ref, dsk_ref, gs_ref,
                o_ref, ht_ref, h_scr, y_scr, *, nsub, nsteps):
    i = pl.program_id(1)

    @pl.when(i == 0)
    def _():
        h_scr[...] = h0_ref[0]

    neg_a = -jnp.exp(alog_ref[...])
    ri = lax.broadcasted_iota(jnp.int32, (CHUNK, CHUNK), 0)
    ci = lax.broadcasted_iota(jnp.int32, (CHUNK, CHUNK), 1)
    tril = jnp.where(ci <= ri, 1.0, 0.0).astype(F32)
    lane2 = lax.broadcasted_iota(jnp.int32, (CHUNK, LANES), 1)
    row2 = lax.broadcasted_iota(jnp.int32, (CHUNK, LANES), 0)
    lo_half = lane2 < CHUNK
    causal2 = jnp.where(lo_half, lane2, lane2 - CHUNK) <= row2
    lane1 = lax.broadcasted_iota(jnp.int32, (1, LANES), 1)
    hm0 = jnp.where(lane1 < SSD_HEAD_DIM, 1.0, 0.0).astype(BF16)
    hm1 = jnp.where(lane1 < SSD_HEAD_DIM, 0.0, 1.0).astype(BF16)
    top_rows = lax.broadcasted_iota(jnp.int32, (2 * SSD_HEAD_DIM, SSD_STATE), 0) < SSD_HEAD_DIM

    for c in range(nsub):
        rows = slice(c * CHUNK, (c + 1) * CHUNK)
        dtc = dt_ref[0, rows, :]
        acum = jnp.dot(tril, dtc * neg_a, precision=lax.Precision.HIGHEST,
                       preferred_element_type=F32)
        acum_t = jnp.concatenate([acum, pltpu.roll(acum, LANES - 1, 1)], axis=0).T
        dt_t = jnp.concatenate([dtc, pltpu.roll(dtc, LANES - 1, 1)], axis=0).T
        a_last = acum[CHUNK - 1:CHUNK, :]
        e_acum = jnp.exp(acum)
        to_end = jnp.exp(a_last - acum) * dtc
        e_last = jnp.exp(a_last)
        bcv = bc_ref[0, rows, :]
        for g in range(SSD_GROUPS):
            bg = bcv[:, g * SSD_STATE:(g + 1) * SSD_STATE]
            cg = bcv[:, (SSD_GROUPS + g) * SSD_STATE:(SSD_GROUPS + g + 1) * SSD_STATE]
            cb2 = lax.dot_general(cg, jnp.concatenate([bg, bg], axis=0), NT_DIMS,
                                  preferred_element_type=F32)
            for j in range(SSD_HEADS // SSD_GROUPS // 2):
                pair = g * (SSD_HEADS // SSD_GROUPS // 2) + j
                e0 = 2 * pair
                blk = slice(pair * LANES, (pair + 1) * LANES)

                def per_head(v):
                    return jnp.where(lo_half, v[:, e0:e0 + 1], v[:, e0 + 1:e0 + 2])

                x2 = xs_ref[0, rows, blk]
                decay = jnp.exp(jnp.where(causal2, per_head(acum) - acum_t[e0:e0 + 1, :], NEG_INF))
                w2 = (cb2 * decay * dt_t[e0:e0 + 1, :]).astype(BF16)
                xbd = jnp.concatenate([x2 * hm0, x2 * hm1], axis=0)
                hp = h_scr[pair]
                y2 = jnp.dot(w2, xbd, preferred_element_type=F32)
                y2 = y2 + lax.dot_general(cg, hp.astype(BF16), NT_DIMS,
                                          preferred_element_type=F32) * per_head(e_acum)
                y_scr[:, blk] = y2
                xe2 = (x2.astype(F32) * per_head(to_end)).astype(BF16)
                hn = lax.dot_general(xe2, bg, TN_DIMS, preferred_element_type=F32)
                scale = jnp.where(top_rows, e_last[:, e0:e0 + 1], e_last[:, e0 + 1:e0 + 2])
                h_scr[pair] = hp * scale + hn

        y = y_scr[...] + xs_ref[0, rows, :].astype(F32) * dsk_ref[...]
        y = y * z_ref[0, rows, :].astype(F32)
        half = SSD_W // SSD_GROUPS
        for g in range(SSD_GROUPS):
            yg = y[:, g * half:(g + 1) * half]
            o_ref[0, rows, g * half:(g + 1) * half] = (
                yg * _rms_scale(yg) * gs_ref[:, g * half:(g + 1) * half]).astype(BF16)

    @pl.when(i == nsteps - 1)
    def _():
        ht_ref[0] = h_scr[...]


def _ssd(xs, bc, dt, zs, h0, lw, *, nsub):
    bt, t, _ = xs.shape
    rows = nsub * CHUNK
    nsteps = t // rows
    tile = lambda w: pl.BlockSpec((1, rows, w), lambda b, i: (b, i, 0))
    const2 = lambda a: pl.BlockSpec(a.shape, lambda b, i: (0, 0))
    npair = SSD_HEADS // 2
    state_spec = pl.BlockSpec((1, npair, 2 * SSD_HEAD_DIM, SSD_STATE), lambda b, i: (b, 0, 0, 0))
    kernel = functools.partial(_ssd_kernel, nsub=nsub, nsteps=nsteps)
    return pl.pallas_call(
        kernel,
        grid=(bt, nsteps),
        in_specs=[tile(SSD_W), tile(CONV_DIM - SSD_W), tile(LANES), tile(SSD_W), state_spec,
                  const2(lw["a_log"]), const2(lw["d_skip"]), const2(lw["g_ssd"])],
        out_specs=[tile(SSD_W), state_spec],
        out_shape=[jax.ShapeDtypeStruct((bt, t, SSD_W), BF16),
                   jax.ShapeDtypeStruct((bt, npair, 2 * SSD_HEAD_DIM, SSD_STATE), F32)],
        scratch_shapes=[pltpu.VMEM((npair, 2 * SSD_HEAD_DIM, SSD_STATE), F32),
                        pltpu.VMEM((CHUNK, SSD_W), F32)],
        compiler_params=pltpu.CompilerParams(
            dimension_semantics=("arbitrary", "arbitrary"), vmem_limit_bytes=VMEM_LIMIT),
        name="ssd",
    )(xs, bc, dt, zs, h0, lw["a_log"], lw["d_skip"], lw["g_ssd"])


def _attn_out_kernel(*refs, tq, n_prev, mask_start):
    q_ref = refs[0]
    k_refs = refs[1:2 + n_prev]
    v_refs = refs[2 + n_prev:3 + 2 * n_prev]
    (ag_ref, s_ref, mq_ref, mg_ref, mk_ref, mv_ref, x_ref, bias_ref, wo_ref, gpost_ref,
     o_ref, kwin, vwin, a_scr, m_scr) = refs[3 + 2 * n_prev:]
    i = pl.program_id(1)

    off = 0
    for kr, vr in zip(k_refs, v_refs):
        n = kr.shape[1]
        kwin[off:off + n, :] = kr[0]
        vwin[off:off + n, :] = vr[0]
        off += n

    lane1 = lax.broadcasted_iota(jnp.int32, (1, LANES), 1)
    hm = (jnp.where(lane1 < ATT_HEAD_DIM, 1.0, 0.0).astype(BF16),
          jnp.where(lane1 < ATT_HEAD_DIM, 0.0, 1.0).astype(BF16))
    lo_half = lax.broadcasted_iota(jnp.int32, (CHUNK, LANES), 1) < ATT_HEAD_DIM
    jidx = lax.broadcasted_iota(jnp.int32, (1, BAND), 1)

    def chunk_body(c, carry):
        r0 = pl.multiple_of(c * CHUNK, CHUNK)
        if mask_start:
            penalty = jnp.where(jidx + (i * tq - ATT_PAST + c * CHUNK) >= 0, 0.0, NEG_INF)
        for j in range(ATT_W // LANES):
            blk = slice(j * LANES, (j + 1) * LANES)
            q2 = q_ref[0, pl.ds(r0, CHUNK), blk]
            k2 = kwin[pl.ds(r0, BAND), blk]
            v2 = vwin[pl.ds(r0, BAND), blk]
            outs = []
            for h2 in range(2):
                s = lax.dot_general(q2 * hm[h2], k2, NT_DIMS, preferred_element_type=F32)
                s = s + bias_ref[2 * j + h2]
                if mask_start:
                    s = s + penalty
                p = jnp.exp(s - jnp.max(s, axis=-1, keepdims=True))
                den = jnp.sum(p, axis=-1, keepdims=True)
                outs.append(jnp.dot(p.astype(BF16), v2, preferred_element_type=F32) * (1.0 / den))
            o2 = jnp.where(lo_half, outs[0], outs[1])
            gate = ag_ref[0, pl.ds(r0, CHUNK), blk].astype(F32)
            a_scr[pl.ds(r0, CHUNK), blk] = (o2 * gate).astype(BF16)
        return carry

    lax.fori_loop(0, tq // CHUNK, chunk_body, 0)

    for hh in range(MEM_HEADS):
        blk = slice(hh * MEM_HEAD_DIM, (hh + 1) * MEM_HEAD_DIM)
        s = lax.dot_general(mq_ref[0, :, blk], mk_ref[0, :, blk], NT_DIMS, preferred_element_type=F32)
        p = jnp.exp(s - jnp.max(s, axis=-1, keepdims=True))
        den = jnp.sum(p, axis=-1, keepdims=True)
        o = jnp.dot(p.astype(BF16), mv_ref[0, :, blk], preferred_element_type=F32) * (1.0 / den)
        m_scr[:, blk] = (o * mg_ref[0, :, blk].astype(F32)).astype(BF16)

    mix = jnp.dot(a_scr[...], wo_ref[0:ATT_W, :], preferred_element_type=F32)
    mix = mix + jnp.dot(s_ref[0], wo_ref[ATT_W:ATT_W + SSD_W, :], preferred_element_type=F32)
    mix = mix + jnp.dot(m_scr[...], wo_ref[ATT_W + SSD_W:D_MIX, :], preferred_element_type=F32)
    o_ref[0] = x_ref[0] + mix * _rms_scale(mix) * gpost_ref[...]


def _attn_out(x, q, k_list, v_list, ag, s_out, mq, mg, mk, mv, lw, *, tq, mask_start):
    bt, t, _ = x.shape
    n_prev = len(k_list) - 1
    tile = lambda w: pl.BlockSpec((1, tq, w), lambda b, i: (b, i, 0))
    const2 = lambda a: pl.BlockSpec(a.shape, lambda b, i: (0, 0))
    const3 = lambda a: pl.BlockSpec(a.shape, lambda b, i: (0, 0, 0))
    win_spec = lambda piece: pl.BlockSpec((1, piece[1], ATT_W), piece[2])
    mem_spec = pl.BlockSpec((1, MEM_LEN, MEM_W), lambda b, i: (b, 0, 0))
    kernel = functools.partial(_attn_out_kernel, tq=tq, n_prev=n_prev, mask_start=mask_start)
    return pl.pallas_call(
        kernel,
        grid=(bt, t // tq),
        in_specs=([tile(ATT_W)] + [win_spec(p) for p in k_list] + [win_spec(p) for p in v_list]
                  + [tile(ATT_W), tile(SSD_W), tile(MEM_W), tile(MEM_W), mem_spec, mem_spec,
                     tile(D_MODEL), const3(lw["bias"]), const2(lw["w_out"]), const2(lw["g_post"])]),
        out_specs=tile(D_MODEL),
        out_shape=jax.ShapeDtypeStruct((bt, t, D_MODEL), F32),
        scratch_shapes=[pltpu.VMEM((ATT_PAST + tq, ATT_W), BF16), pltpu.VMEM((ATT_PAST + tq, ATT_W), BF16),
                        pltpu.VMEM((tq, ATT_W), BF16), pltpu.VMEM((tq, MEM_W), BF16)],
        compiler_params=pltpu.CompilerParams(
            dimension_semantics=("arbitrary", "arbitrary"), vmem_limit_bytes=VMEM_LIMIT),
        name="attn_out",
    )(q, *[p[0] for p in k_list], *[p[0] for p in v_list], ag, s_out, mq, mg, mk, mv, x,
      lw["bias"], lw["w_out"], lw["g_post"])


def _trunk_layer(x, mk, mv, conv_state8, h0, k_cache, v_cache, lw, *, tm, tq, nsub, keep):
    (q, k, v, ag, zs, xs, bc, dt, mq, mg, krow, vrow, crow) = _inproj(x, conv_state8, lw, tm=tm, keep=keep)
    s_out, h_t = _ssd(xs, bc, dt, zs, h0, lw, nsub=nsub)
    if k_cache is None:
        nprev = ATT_PAST // tq
        prev = [(lambda b, i, d=d: (b, jnp.maximum(i - d, 0), 0)) for d in range(nprev, 0, -1)]
        cur = lambda b, i: (b, i, 0)
        k_list = [(k, tq, m) for m in prev] + [(k, tq, cur)]
        v_list = [(v, tq, m) for m in prev] + [(v, tq, cur)]
        mask_start = True
    else:
        whole = lambda b, i: (b, 0, 0)
        k_list = [(k_cache, ATT_PAST, whole), (k, tq, whole)]
        v_list = [(v_cache, ATT_PAST, whole), (v, tq, whole)]
        mask_start = False
    y = _attn_out(x, q, k_list, v_list, ag, s_out, mq, mg, mk, mv, lw, tq=tq, mask_start=mask_start)
    return y, krow, vrow, crow, h_t


def _rel_bias_table(rel_bias):
    dist = jnp.arange(CHUNK)[:, None] + ATT_PAST - jnp.arange(BAND)[None, :]
    return rel_bias[:, :, jnp.clip(dist, -REL_CLIP, REL_CLIP) + REL_CLIP].astype(F32)


def kernel(x_prompt, x_sample, cache_att_k, cache_att_v, state_conv, state_ssd, cache_mem_k, cache_mem_v,
           mem_prompt, w_in, w_out, w_mem_kv, norm_pre, norm_post, norm_mem, rel_bias, conv_w, conv_b,
           dt_bias, a_log, d_skip, norm_ssd):
    bp, seq, _ = x_prompt.shape
    bs, tdec, _ = x_sample.shape
    assert tdec == CHUNK and cache_att_k.shape[2] == ATT_PAST and seq % 256 == 0

    dt_lo = 4 * ATT_W + SSD_W + CONV_DIM
    w_main = jnp.concatenate([w_in[:, :, :dt_lo], w_in[:, :, dt_lo + SSD_HEADS:]], axis=-1).astype(BF16)
    w_dt = jnp.pad(w_in[:, :, dt_lo:dt_lo + SSD_HEADS], ((0, 0), (0, 0), (0, LANES - SSD_HEADS))).astype(BF16)
    w_out_bf = w_out.astype(BF16)
    pad_heads = lambda a: jnp.pad(a, ((0, 0), (0, LANES - SSD_HEADS)))[:, None, :]
    dt_bias_p = pad_heads(dt_bias)
    a_log_p = pad_heads(a_log)
    d_skip_x = jnp.repeat(d_skip, SSD_HEAD_DIM, axis=-1)[:, None, :]
    bias_tab = _rel_bias_table(rel_bias)

    mkv32, mkv16 = _memkv(mem_prompt, norm_mem, w_mem_kv.astype(BF16))

    pad_state = lambda s: jnp.pad(s, ((0, 0), (SUBLANES - (CONV_K - 1), 0), (0, 0)))
    npair = SSD_HEADS // 2
    conv0 = jnp.zeros((bp, SUBLANES, CONV_DIM), F32)
    ssd0 = jnp.zeros((bp, npair, 2 * SSD_HEAD_DIM, SSD_STATE), F32)

    hp, hs = x_prompt, x_sample
    outs = {n: [] for n in ("pk", "pv", "pc", "ps", "sk", "sv", "sc", "ss")}
    for l in range(DEPTH):
        lw = dict(g_pre=norm_pre[l][None, :], w_main=w_main[l], w_dt=w_dt[l], conv_w=conv_w[l],
                  conv_b=conv_b[l][None, :], dt_bias=dt_bias_p[l], a_log=a_log_p[l], d_skip=d_skip_x[l],
                  g_ssd=norm_ssd[l][None, :], bias=bias_tab[l], w_out=w_out_bf[l], g_post=norm_post[l][None, :])
        hp, ka, va, ca, sa = _trunk_layer(
            hp, mkv16[l, :, :, :MEM_W], mkv16[l, :, :, MEM_W:], conv0, ssd0, None, None, lw,
            tm=256, tq=256, nsub=4, keep=ATT_PAST)
        outs["pk"].append(ka); outs["pv"].append(va); outs["pc"].append(ca); outs["ps"].append(sa)
        hs, kb, vb, cb, sb = _trunk_layer(
            hs, cache_mem_k[l].reshape(bs, MEM_LEN, MEM_W).astype(BF16),
            cache_mem_v[l].reshape(bs, MEM_LEN, MEM_W).astype(BF16),
            pad_state(state_conv[l]), state_ssd[l].reshape(bs, npair, 2 * SSD_HEAD_DIM, SSD_STATE),
            cache_att_k[l].reshape(bs, ATT_PAST, ATT_W).astype(BF16),
            cache_att_v[l].reshape(bs, ATT_PAST, ATT_W).astype(BF16), lw,
            tm=CHUNK, tq=CHUNK, nsub=1, keep=CHUNK)
        outs["sk"].append(kb); outs["sv"].append(vb); outs["sc"].append(cb); outs["ss"].append(sb)

    heads = lambda rows, b: jnp.stack(rows).reshape(DEPTH, b, -1, ATT_HEADS, ATT_HEAD_DIM)
    conv_rows = lambda rows: jnp.stack(rows)[:, :, SUBLANES - (CONV_K - 1):, :]
    ssd_state = lambda rows, b: jnp.stack(rows).reshape(DEPTH, b, SSD_HEADS, SSD_HEAD_DIM, SSD_STATE)
    p_mem_k = mkv32[..., :MEM_W].reshape(DEPTH, bp, MEM_LEN, MEM_HEADS, MEM_HEAD_DIM)
    p_mem_v = mkv32[..., MEM_W:].reshape(DEPTH, bp, MEM_LEN, MEM_HEADS, MEM_HEAD_DIM)
    return (hp, hs, heads(outs["pk"], bp), heads(outs["pv"], bp), conv_rows(outs["pc"]),
            ssd_state(outs["ps"], bp), p_mem_k, p_mem_v, heads(outs["sk"], bs), heads(outs["sv"], bs),
            conv_rows(outs["sc"]), ssd_state(outs["ss"], bs))
```

```python
import functools

import jax
import jax.numpy as jnp
from jax import lax
from jax.experimental import pallas as pl
from jax.experimental.pallas import tpu as pltpu

F32 = jnp.float32
BF16 = jnp.bfloat16
NEG_INF = float("-inf")

D_MODEL = 1024
DEPTH = 4
CHUNK = 64
EPS = 1e-6
ATT_HEADS = 8
ATT_HEAD_DIM = 64
ATT_W = ATT_HEADS * ATT_HEAD_DIM
ATT_PAST = 8 * CHUNK
BAND = ATT_PAST + CHUNK
REL_CLIP = 128
SSD_HEADS = 16
SSD_HEAD_DIM = 64
SSD_W = SSD_HEADS * SSD_HEAD_DIM
SSD_GROUPS = 2
SSD_STATE = 128
CONV_K = 4
CONV_DIM = SSD_W + 2 * SSD_GROUPS * SSD_STATE
MEM_LEN = 256
MEM_HEADS = 4
MEM_HEAD_DIM = 128
MEM_W = MEM_HEADS * MEM_HEAD_DIM
D_MIX = ATT_W + SSD_W + MEM_W
ATT_SCALE = ATT_HEAD_DIM ** -0.5
MEM_SCALE = MEM_HEAD_DIM ** -0.5

LANES = 128
SUBLANES = 8
VMEM_LIMIT = 56 * 1024 * 1024

C_Q, C_K, C_V, C_AG = 0, ATT_W, 2 * ATT_W, 3 * ATT_W
C_Z = 4 * ATT_W
C_XBC = C_Z + SSD_W
C_MQ = C_XBC + CONV_DIM
C_MG = C_MQ + MEM_W
C_END = C_MG + MEM_W

NT_DIMS = (((1,), (1,)), ((), ()))
TN_DIMS = (((0,), (0,)), ((), ()))


def _silu(y):
    return y * (1.0 / (1.0 + jnp.exp(-y)))


def _rms_scale(x):
    return lax.rsqrt(jnp.mean(x * x, axis=-1, keepdims=True) + EPS)


def _memkv_kernel(x_ref, g_ref, w_ref, o32_ref, o16_ref):
    x = x_ref[0]
    h = ((x * _rms_scale(x)) * g_ref[0]).astype(BF16)
    kv = jnp.dot(h, w_ref[0], preferred_element_type=F32)
    o32_ref[0, 0] = kv
    o16_ref[0, 0] = kv.astype(BF16)


def _memkv(mem_prompt, norm_mem, w_mem_kv_bf):
    bp = mem_prompt.shape[0]
    return pl.pallas_call(
        _memkv_kernel,
        grid=(DEPTH, bp),
        in_specs=[
            pl.BlockSpec((1, MEM_LEN, D_MODEL), lambda l, b: (b, 0, 0)),
            pl.BlockSpec((1, 1, D_MODEL), lambda l, b: (l, 0, 0)),
            pl.BlockSpec((1, D_MODEL, 2 * MEM_W), lambda l, b: (l, 0, 0)),
        ],
        out_specs=[
            pl.BlockSpec((1, 1, MEM_LEN, 2 * MEM_W), lambda l, b: (l, b, 0, 0)),
            pl.BlockSpec((1, 1, MEM_LEN, 2 * MEM_W), lambda l, b: (l, b, 0, 0)),
        ],
        out_shape=[
            jax.ShapeDtypeStruct((DEPTH, bp, MEM_LEN, 2 * MEM_W), F32),
            jax.ShapeDtypeStruct((DEPTH, bp, MEM_LEN, 2 * MEM_W), BF16),
        ],
        compiler_params=pltpu.CompilerParams(dimension_semantics=("arbitrary", "arbitrary")),
        name="memkv",
    )(mem_prompt, norm_mem.reshape(DEPTH, 1, D_MODEL), w_mem_kv_bf)


def _inproj_kernel(x_ref, g_ref, w_ref, wdt_ref, cs_ref, cw_ref, cb_ref, dtb_ref,
                   q_ref, k_ref, v_ref, ag_ref, z_ref, xs_ref, bc_ref, dt_ref, mq_ref, mg_ref,
                   krow_ref, vrow_ref, crow_ref, ext_ref, *, tm, nt, keep_tiles):
    i = pl.program_id(1)
    x = x_ref[0]
    h = ((x * _rms_scale(x)) * g_ref[...]).astype(BF16)

    def proj(lo, hi):
        return jnp.dot(h, w_ref[:, lo:hi], preferred_element_type=F32)

    q_ref[0] = (proj(C_Q, C_K) * ATT_SCALE).astype(BF16)
    kf = proj(C_K, C_V)
    vf = proj(C_V, C_AG)
    k_ref[0] = kf.astype(BF16)
    v_ref[0] = vf.astype(BF16)

    @pl.when(i >= nt - keep_tiles)
    def _():
        off = pl.multiple_of((i - (nt - keep_tiles)) * tm, tm)
        krow_ref[0, pl.ds(off, tm), :] = kf
        vrow_ref[0, pl.ds(off, tm), :] = vf

    ag_ref[0] = _silu(proj(C_AG, C_Z)).astype(BF16)
    z_ref[0] = _silu(proj(C_Z, C_XBC)).astype(BF16)
    mq_ref[0] = (proj(C_MQ, C_MG) * MEM_SCALE).astype(BF16)
    mg_ref[0] = _silu(proj(C_MG, C_END)).astype(BF16)

    dtr = jnp.dot(h, wdt_ref[...], preferred_element_type=F32) + dtb_ref[...]
    dt_ref[0] = jnp.maximum(dtr, 0.0) + jnp.log1p(jnp.exp(-jnp.abs(dtr)))

    xbc = proj(C_XBC, C_MQ)

    @pl.when(i == 0)
    def _():
        ext_ref[0:SUBLANES, :] = cs_ref[0]

    ext_ref[SUBLANES:SUBLANES + tm, :] = xbc
    y = cb_ref[...] + xbc * cw_ref[CONV_K - 1:CONV_K, :]
    for tap in range(CONV_K - 1):
        lo = SUBLANES - (CONV_K - 1) + tap
        y = y + ext_ref[lo:lo + tm, :] * cw_ref[tap:tap + 1, :]
    tail = xbc[tm - SUBLANES:tm, :]
    ext_ref[0:SUBLANES, :] = tail
    crow_ref[0] = tail
    xc = _silu(y)
    xs_ref[0] = xc[:, 0:SSD_W].astype(BF16)
    bc_ref[0] = xc[:, SSD_W:CONV_DIM].astype(BF16)


def _inproj(x, conv_state8, lw, *, tm, keep):
    bt, t, _ = x.shape
    nt = t // tm
    keep_tiles = keep // tm
    tile = lambda w: pl.BlockSpec((1, tm, w), lambda b, i: (b, i, 0))
    const2 = lambda a: pl.BlockSpec(a.shape, lambda b, i: (0, 0))
    bf = lambda w: jax.ShapeDtypeStruct((bt, t, w), BF16)
    bc_w = CONV_DIM - SSD_W
    kernel = functools.partial(_inproj_kernel, tm=tm, nt=nt, keep_tiles=keep_tiles)
    return pl.pallas_call(
        kernel,
        grid=(bt, nt),
        in_specs=[
            tile(D_MODEL),
            const2(lw["g_pre"]), const2(lw["w_main"]), const2(lw["w_dt"]),
            pl.BlockSpec((1, SUBLANES, CONV_DIM), lambda b, i: (b, 0, 0)),
            const2(lw["conv_w"]), const2(lw["conv_b"]), const2(lw["dt_bias"]),
        ],
        out_specs=[
            tile(ATT_W), tile(ATT_W), tile(ATT_W), tile(ATT_W), tile(SSD_W), tile(SSD_W), tile(bc_w),
            tile(LANES), tile(MEM_W), tile(MEM_W),
            pl.BlockSpec((1, keep, ATT_W), lambda b, i: (b, 0, 0)),
            pl.BlockSpec((1, keep, ATT_W), lambda b, i: (b, 0, 0)),
            pl.BlockSpec((1, SUBLANES, CONV_DIM), lambda b, i: (b, 0, 0)),
        ],
        out_shape=[
            bf(ATT_W), bf(ATT_W), bf(ATT_W), bf(ATT_W), bf(SSD_W), bf(SSD_W), bf(bc_w),
            jax.ShapeDtypeStruct((bt, t, LANES), F32), bf(MEM_W), bf(MEM_W),
            jax.ShapeDtypeStruct((bt, keep, ATT_W), F32),
            jax.ShapeDtypeStruct((bt, keep, ATT_W), F32),
            jax.ShapeDtypeStruct((bt, SUBLANES, CONV_DIM), F32),
        ],
        scratch_shapes=[pltpu.VMEM((SUBLANES + tm, CONV_DIM), F32)],
        compiler_params=pltpu.CompilerParams(
            dimension_semantics=("arbitrary", "arbitrary"), vmem_limit_bytes=VMEM_LIMIT),
        name="inproj",
    )(x, lw["g_pre"], lw["w_main"], lw["w_dt"], conv_state8, lw["conv_w"], lw["conv_b"], lw["dt_bias"])


def _ssd_kernel(xs_ref, bc_ref, dt_ref, z_ref, h0_ref, alog_ref, dsk_ref, gs_ref,
                o_ref, ht_ref, h_scr, y_scr, *, nsub, nsteps):
    i = pl.program_id(1)

    @pl.when(i == 0)
    def _():
        h_scr[...] = h0_ref[0]

    neg_a = -jnp.exp(alog_ref[...])
    ri = lax.broadcasted_iota(jnp.int32, (CHUNK, CHUNK), 0)
    ci = lax.broadcasted_iota(jnp.int32, (CHUNK, CHUNK), 1)
    tril = jnp.where(ci <= ri, 1.0, 0.0).astype(F32)
    lane2 = lax.broadcasted_iota(jnp.int32, (CHUNK, LANES), 1)
    row2 = lax.broadcasted_iota(jnp.int32, (CHUNK, LANES), 0)
    lo_half = lane2 < CHUNK
    causal2 = jnp.where(lo_half, lane2, lane2 - CHUNK) <= row2
    lane1 = lax.broadcasted_iota(jnp.int32, (1, LANES), 1)
    hm0 = jnp.where(lane1 < SSD_HEAD_DIM, 1.0, 0.0).astype(BF16)
    hm1 = jnp.where(lane1 < SSD_HEAD_DIM, 0.0, 1.0).astype(BF16)
    top_rows = lax.broadcasted_iota(jnp.int32, (2 * SSD_HEAD_DIM, SSD_STATE), 0) < SSD_HEAD_DIM

    for c in range(nsub):
        rows = slice(c * CHUNK, (c + 1) * CHUNK)
        dtc = dt_ref[0, rows, :]
        acum = jnp.dot(tril, dtc * neg_a, precision=lax.Precision.HIGHEST,
                       preferred_element_type=F32)
        acum_t = jnp.concatenate([acum, pltpu.roll(acum, LANES - 1, 1)], axis=0).T
        dt_t = jnp.concatenate([dtc, pltpu.roll(dtc, LANES - 1, 1)], axis=0).T
        a_last = acum[CHUNK - 1:CHUNK, :]
        e_acum = jnp.exp(acum)
        to_end = jnp.exp(a_last - acum) * dtc
        e_last = jnp.exp(a_last)
        bcv = bc_ref[0, rows, :]
        for g in range(SSD_GROUPS):
            bg = bcv[:, g * SSD_STATE:(g + 1) * SSD_STATE]
            cg = bcv[:, (SSD_GROUPS + g) * SSD_STATE:(SSD_GROUPS + g + 1) * SSD_STATE]
            cb2 = lax.dot_general(cg, jnp.concatenate([bg, bg], axis=0), NT_DIMS,
                                  preferred_element_type=F32)
            for j in range(SSD_HEADS // SSD_GROUPS // 2):
                pair = g * (SSD_HEADS // SSD_GROUPS // 2) + j
                e0 = 2 * pair
                blk = slice(pair * LANES, (pair + 1) * LANES)

                def per_head(v):
                    return jnp.where(lo_half, v[:, e0:e0 + 1], v[:, e0 + 1:e0 + 2])

                x2 = xs_ref[0, rows, blk]
                decay = jnp.exp(jnp.where(causal2, per_head(acum) - acum_t[e0:e0 + 1, :], NEG_INF))
                w2 = (cb2 * decay * dt_t[e0:e0 + 1, :]).astype(BF16)
                xbd = jnp.concatenate([x2 * hm0, x2 * hm1], axis=0)
                hp = h_scr[pair]
                y2 = jnp.dot(w2, xbd, preferred_element_type=F32)
                y2 = y2 + lax.dot_general(cg, hp.astype(BF16), NT_DIMS,
                                          preferred_element_type=F32) * per_head(e_acum)
                y_scr[:, blk] = y2
                xe2 = (x2.astype(F32) * per_head(to_end)).astype(BF16)
                hn = lax.dot_general(xe2, bg, TN_DIMS, preferred_element_type=F32)
                scale = jnp.where(top_rows, e_last[:, e0:e0 + 1], e_last[:, e0 + 1:e0 + 2])
                h_scr[pair] = hp * scale + hn

        y = y_scr[...] + xs_ref[0, rows, :].astype(F32) * dsk_ref[...]
        y = y * z_ref[0, rows, :].astype(F32)
        half = SSD_W // SSD_GROUPS
        for g in range(SSD_GROUPS):
            yg = y[:, g * half:(g + 1) * half]
            o_ref[0, rows, g * half:(g + 1) * half] = (
                yg * _rms_scale(yg) * gs_ref[:, g * half:(g + 1) * half]).astype(BF16)

    @pl.when(i == nsteps - 1)
    def _():
        ht_ref[0] = h_scr[...]


def _ssd(xs, bc, dt, zs, h0, lw, *, nsub):
    bt, t, _ = xs.shape
    rows = nsub * CHUNK
    nsteps = t // rows
    tile = lambda w: pl.BlockSpec((1, rows, w), lambda b, i: (b, i, 0))
    const2 = lambda a: pl.BlockSpec(a.shape, lambda b, i: (0, 0))
    npair = SSD_HEADS // 2
    state_spec = pl.BlockSpec((1, npair, 2 * SSD_HEAD_DIM, SSD_STATE), lambda b, i: (b, 0, 0, 0))
    kernel = functools.partial(_ssd_kernel, nsub=nsub, nsteps=nsteps)
    return pl.pallas_call(
        kernel,
        grid=(bt, nsteps),
        in_specs=[tile(SSD_W), tile(CONV_DIM - SSD_W), tile(LANES), tile(SSD_W), state_spec,
                  const2(lw["a_log"]), const2(lw["d_skip"]), const2(lw["g_ssd"])],
        out_specs=[tile(SSD_W), state_spec],
        out_shape=[jax.ShapeDtypeStruct((bt, t, SSD_W), BF16),
                   jax.ShapeDtypeStruct((bt, npair, 2 * SSD_HEAD_DIM, SSD_STATE), F32)],
        scratch_shapes=[pltpu.VMEM((npair, 2 * SSD_HEAD_DIM, SSD_STATE), F32),
                        pltpu.VMEM((CHUNK, SSD_W), F32)],
        compiler_params=pltpu.CompilerParams(
            dimension_semantics=("arbitrary", "arbitrary"), vmem_limit_bytes=VMEM_LIMIT),
        name="ssd",
    )(xs, bc, dt, zs, h0, lw["a_log"], lw["d_skip"], lw["g_ssd"])


def _attn_out_kernel(*refs, tq, n_prev, mask_start):
    q_ref = refs[0]
    k_refs = refs[1:2 + n_prev]
    v_refs = refs[2 + n_prev:3 + 2 * n_prev]
    (ag_ref, s_ref, mq_ref, mg_ref, mk_ref, mv_ref, x_ref, bias_ref, wo_ref, gpost_ref,
     o_ref, kwin, vwin, a_scr, m_scr) = refs[3 + 2 * n_prev:18 + 2 * n_prev]
    if mask_start:
        bias_scr = refs[18 + 2 * n_prev]
    i = pl.program_id(1)

    off = 0
    for kr, vr in zip(k_refs, v_refs):
        n = kr.shape[-2]
        lead = (0,) * (len(kr.shape) - 2)
        kwin[off:off + n, :] = kr[lead]
        vwin[off:off + n, :] = vr[lead]
        off += n

    lane1 = lax.broadcasted_iota(jnp.int32, (1, LANES), 1)
    hm = (jnp.where(lane1 < ATT_HEAD_DIM, 1.0, 0.0).astype(BF16),
          jnp.where(lane1 < ATT_HEAD_DIM, 0.0, 1.0).astype(BF16))
    lo_half = lax.broadcasted_iota(jnp.int32, (tq, LANES), 1) < ATT_HEAD_DIM

    if mask_start:
        @pl.when(i <= n_prev)
        def _():
            widx = lax.broadcasted_iota(jnp.int32, (1, ATT_PAST + tq), 1)
            penalty = jnp.where(widx + (i * tq - ATT_PAST) >= 0, 0.0, NEG_INF)
            for hd in range(ATT_HEADS):
                bias_scr[hd] = bias_ref[hd] + penalty
        bias_src = bias_scr
    else:
        bias_src = bias_ref

    for j in range(ATT_W // LANES):
        blk = slice(j * LANES, (j + 1) * LANES)
        q2 = q_ref[0, :, blk]
        k2 = kwin[:, blk]
        v2 = vwin[:, blk]
        outs = []
        for h2 in range(2):
            s = lax.dot_general(q2 * hm[h2], k2, NT_DIMS, preferred_element_type=F32)
            s = s + bias_src[2 * j + h2]
            p = jnp.exp(s - jnp.max(s, axis=-1, keepdims=True))
            den = jnp.sum(p, axis=-1, keepdims=True)
            outs.append(jnp.dot(p.astype(BF16), v2, preferred_element_type=F32) * (1.0 / den))
        o2 = jnp.where(lo_half, outs[0], outs[1])
        a_scr[:, blk] = (o2 * ag_ref[0, :, blk].astype(F32)).astype(BF16)

    for hh in range(MEM_HEADS):
        blk = slice(hh * MEM_HEAD_DIM, (hh + 1) * MEM_HEAD_DIM)
        s = lax.dot_general(mq_ref[0, :, blk], mk_ref[0, 0, :, blk], NT_DIMS, preferred_element_type=F32)
        p = jnp.exp(s - jnp.max(s, axis=-1, keepdims=True))
        den = jnp.sum(p, axis=-1, keepdims=True)
        o = jnp.dot(p.astype(BF16), mv_ref[0, 0, :, blk], preferred_element_type=F32) * (1.0 / den)
        m_scr[:, blk] = (o * mg_ref[0, :, blk].astype(F32)).astype(BF16)

    mix = jnp.dot(a_scr[...], wo_ref[0:ATT_W, :], preferred_element_type=F32)
    mix = mix + jnp.dot(s_ref[0], wo_ref[ATT_W:ATT_W + SSD_W, :], preferred_element_type=F32)
    mix = mix + jnp.dot(m_scr[...], wo_ref[ATT_W + SSD_W:D_MIX, :], preferred_element_type=F32)
    o_ref[0] = x_ref[0] + mix * _rms_scale(mix) * gpost_ref[...]


def _attn_out(x, q, k_list, v_list, ag, s_out, mq, mg, mk, mv, bias, lw, *, tq, mask_start):
    bt, t, _ = x.shape
    n_prev = len(k_list) - 1
    tile = lambda w: pl.BlockSpec((1, tq, w), lambda b, i: (b, i, 0))
    const2 = lambda a: pl.BlockSpec(a.shape, lambda b, i: (0, 0))
    const3 = lambda a: pl.BlockSpec(a.shape, lambda b, i: (0, 0, 0))
    win_spec = lambda piece: pl.BlockSpec(piece[1], piece[2])
    mem_spec = lambda m: pl.BlockSpec((1, 1, MEM_LEN, MEM_W), m[1])
    kernel = functools.partial(_attn_out_kernel, tq=tq, n_prev=n_prev, mask_start=mask_start)
    return pl.pallas_call(
        kernel,
        grid=(bt, t // tq),
        in_specs=([tile(ATT_W)] + [win_spec(p) for p in k_list] + [win_spec(p) for p in v_list]
                  + [tile(ATT_W), tile(SSD_W), tile(MEM_W), tile(MEM_W), mem_spec(mk), mem_spec(mv),
                     tile(D_MODEL), const3(bias), const2(lw["w_out"]), const2(lw["g_post"])]),
        out_specs=tile(D_MODEL),
        out_shape=jax.ShapeDtypeStruct((bt, t, D_MODEL), F32),
        scratch_shapes=([pltpu.VMEM((ATT_PAST + tq, ATT_W), BF16), pltpu.VMEM((ATT_PAST + tq, ATT_W), BF16),
                         pltpu.VMEM((tq, ATT_W), BF16), pltpu.VMEM((tq, MEM_W), BF16)]
                        + ([pltpu.VMEM(bias.shape, F32)] if mask_start else [])),
        compiler_params=pltpu.CompilerParams(
            dimension_semantics=("arbitrary", "arbitrary"), vmem_limit_bytes=VMEM_LIMIT),
        name="attn_out",
    )(q, *[p[0] for p in k_list], *[p[0] for p in v_list], ag, s_out, mq, mg, mk[0], mv[0], x,
      bias, lw["w_out"], lw["g_post"])


def _trunk_layer(x, mk, mv, conv_state8, h0, kv_cache, bias, lw, *, tm, tq, nsub, keep):
    (q, k, v, ag, zs, xs, bc, dt, mq, mg, krow, vrow, crow) = _inproj(x, conv_state8, lw, tm=tm, keep=keep)
    s_out, h_t = _ssd(xs, bc, dt, zs, h0, lw, nsub=nsub)
    blk = (1, tq, ATT_W)
    if kv_cache is None:
        nprev = ATT_PAST // tq
        prev = [(lambda b, i, d=d: (b, jnp.maximum(i - d, 0), 0)) for d in range(nprev, 0, -1)]
        cur = lambda b, i: (b, i, 0)
        k_list = [(k, blk, m) for m in prev] + [(k, blk, cur)]
        v_list = [(v, blk, m) for m in prev] + [(v, blk, cur)]
    else:
        whole = lambda b, i: (b, 0, 0)
        k_list = [kv_cache[0], (k, blk, whole)]
        v_list = [kv_cache[1], (v, blk, whole)]
    y = _attn_out(x, q, k_list, v_list, ag, s_out, mq, mg, mk, mv, bias, lw, tq=tq,
                  mask_start=kv_cache is None)
    return y, krow, vrow, crow, h_t


def _rel_bias_table(rel_bias, tq):
    ncols = ATT_PAST + tq
    span = tq + ncols - 1
    dist = jnp.arange(span) - (ncols - 1) + ATT_PAST
    per_dist = rel_bias[:, :, jnp.clip(dist, -REL_CLIP, REL_CLIP) + REL_CLIP].astype(F32)
    skew = jnp.tile(per_dist, (1, 1, tq + 1))[:, :, :tq * (span + 1)]
    table = skew.reshape(DEPTH, ATT_HEADS, tq, span + 1)[:, :, :, :ncols][:, :, :, ::-1]
    q_chunk = jnp.arange(tq)[:, None] // CHUNK
    k_chunk = jnp.arange(ncols)[None, :] // CHUNK
    in_band = (k_chunk >= q_chunk) & (k_chunk <= q_chunk + ATT_PAST // CHUNK)
    return jnp.where(in_band, table, NEG_INF)


def kernel(x_prompt, x_sample, cache_att_k, cache_att_v, state_conv, state_ssd, cache_mem_k, cache_mem_v,
           mem_prompt, w_in, w_out, w_mem_kv, norm_pre, norm_post, norm_mem, rel_bias, conv_w, conv_b,
           dt_bias, a_log, d_skip, norm_ssd):
    bp, seq, _ = x_prompt.shape
    bs, tdec, _ = x_sample.shape
    assert tdec == CHUNK and cache_att_k.shape[2] == ATT_PAST and seq % 256 == 0

    dt_lo = 4 * ATT_W + SSD_W + CONV_DIM
    w_main = jnp.concatenate([w_in[:, :, :dt_lo], w_in[:, :, dt_lo + SSD_HEADS:]], axis=-1).astype(BF16)
    w_dt = jnp.pad(w_in[:, :, dt_lo:dt_lo + SSD_HEADS], ((0, 0), (0, 0), (0, LANES - SSD_HEADS))).astype(BF16)
    w_out_bf = w_out.astype(BF16)
    pad_heads = lambda a: jnp.pad(a, ((0, 0), (0, LANES - SSD_HEADS)))[:, None, :]
    dt_bias_p = pad_heads(dt_bias)
    a_log_p = pad_heads(a_log)
    d_skip_x = jnp.repeat(d_skip, SSD_HEAD_DIM, axis=-1)[:, None, :]
    tq_prompt = 256
    bias_prompt = _rel_bias_table(rel_bias, tq_prompt)
    bias_sample = _rel_bias_table(rel_bias, CHUNK)

    mkv32, mkv16 = _memkv(mem_prompt, norm_mem, w_mem_kv.astype(BF16))
    cmem_k = cache_mem_k.reshape(DEPTH, bs, MEM_LEN, MEM_W).astype(BF16)
    cmem_v = cache_mem_v.reshape(DEPTH, bs, MEM_LEN, MEM_W).astype(BF16)
    catt_k = cache_att_k.reshape(DEPTH, bs, ATT_PAST, ATT_W).astype(BF16)
    catt_v = cache_att_v.reshape(DEPTH, bs, ATT_PAST, ATT_W).astype(BF16)

    pad_state = lambda s: jnp.pad(s, ((0, 0), (SUBLANES - (CONV_K - 1), 0), (0, 0)))
    npair = SSD_HEADS // 2
    conv0 = jnp.zeros((bp, SUBLANES, CONV_DIM), F32)
    ssd0 = jnp.zeros((bp, npair, 2 * SSD_HEAD_DIM, SSD_STATE), F32)

    hp, hs = x_prompt, x_sample
    outs = {n: [] for n in ("pk", "pv", "pc", "ps", "sk", "sv", "sc", "ss")}
    for l in range(DEPTH):
        lw = dict(g_pre=norm_pre[l][None, :], w_main=w_main[l], w_dt=w_dt[l], conv_w=conv_w[l],
                  conv_b=conv_b[l][None, :], dt_bias=dt_bias_p[l], a_log=a_log_p[l], d_skip=d_skip_x[l],
                  g_ssd=norm_ssd[l][None, :], w_out=w_out_bf[l], g_post=norm_post[l][None, :])
        layer_blk = lambda b, i, l=l: (l, b, 0, 0)
        hp, ka, va, ca, sa = _trunk_layer(
            hp, (mkv16, layer_blk), (mkv16, lambda b, i, l=l: (l, b, 0, 1)), conv0, ssd0, None,
            bias_prompt[l], lw, tm=256, tq=tq_prompt, nsub=4, keep=ATT_PAST)
        outs["pk"].append(ka); outs["pv"].append(va); outs["pc"].append(ca); outs["ps"].append(sa)
        cache_blk = (1, 1, ATT_PAST, ATT_W)
        hs, kb, vb, cb, sb = _trunk_layer(
            hs, (cmem_k, layer_blk), (cmem_v, layer_blk),
            pad_state(state_conv[l]), state_ssd[l].reshape(bs, npair, 2 * SSD_HEAD_DIM, SSD_STATE),
            ((catt_k, cache_blk, layer_blk), (catt_v, cache_blk, layer_blk)),
            bias_sample[l], lw, tm=CHUNK, tq=CHUNK, nsub=1, keep=CHUNK)
        outs["sk"].append(kb); outs["sv"].append(vb); outs["sc"].append(cb); outs["ss"].append(sb)

    heads = lambda rows, b: jnp.stack(rows).reshape(DEPTH, b, -1, ATT_HEADS, ATT_HEAD_DIM)
    conv_rows = lambda rows: jnp.stack(rows)[:, :, SUBLANES - (CONV_K - 1):, :]
    ssd_state = lambda rows, b: jnp.stack(rows).reshape(DEPTH, b, SSD_HEADS, SSD_HEAD_DIM, SSD_STATE)
    p_mem_k = mkv32[..., :MEM_W].reshape(DEPTH, bp, MEM_LEN, MEM_HEADS, MEM_HEAD_DIM)
    p_mem_v = mkv32[..., MEM_W:].reshape(DEPTH, bp, MEM_LEN, MEM_HEADS, MEM_HEAD_DIM)
    return (hp, hs, heads(outs["pk"], bp), heads(outs["pv"], bp), conv_rows(outs["pc"]),
            ssd_state(outs["ps"], bp), p_mem_k, p_mem_v, heads(outs["sk"], bs), heads(outs["sv"], bs),
            conv_rows(outs["sc"]), ssd_state(outs["ss"], bs))
```

```python
import functools

import jax
import jax.numpy as jnp
from jax import lax
from jax.experimental import pallas as pl
from jax.experimental.pallas import tpu as pltpu

F32 = jnp.float32
BF16 = jnp.bfloat16
NEG_INF = float("-inf")

D_MODEL = 1024
DEPTH = 4
CHUNK = 64
EPS = 1e-6
ATT_HEADS = 8
ATT_HEAD_DIM = 64
ATT_W = ATT_HEADS * ATT_HEAD_DIM
ATT_PAST = 8 * CHUNK
BAND = ATT_PAST + CHUNK
REL_CLIP = 128
SSD_HEADS = 16
SSD_HEAD_DIM = 64
SSD_W = SSD_HEADS * SSD_HEAD_DIM
SSD_GROUPS = 2
SSD_STATE = 128
CONV_K = 4
CONV_DIM = SSD_W + 2 * SSD_GROUPS * SSD_STATE
MEM_LEN = 256
MEM_HEADS = 4
MEM_HEAD_DIM = 128
MEM_W = MEM_HEADS * MEM_HEAD_DIM
D_MIX = ATT_W + SSD_W + MEM_W
ATT_SCALE = ATT_HEAD_DIM ** -0.5
MEM_SCALE = MEM_HEAD_DIM ** -0.5

LANES = 128
SUBLANES = 8
CONV_ROWS = 32
CONV_COLS = 256
SOFTMAX_ROWS = 32
VMEM_LIMIT = 56 * 1024 * 1024

C_Q, C_K, C_V, C_AG = 0, ATT_W, 2 * ATT_W, 3 * ATT_W
C_Z = 4 * ATT_W
C_XBC = C_Z + SSD_W
C_MQ = C_XBC + CONV_DIM
C_MG = C_MQ + MEM_W
C_END = C_MG + MEM_W

NT_DIMS = (((1,), (1,)), ((), ()))
TN_DIMS = (((0,), (0,)), ((), ()))


def _silu(y):
    t = 0.5 * y
    return t * jnp.tanh(t) + t


def _rms_scale(x):
    return lax.rsqrt(jnp.mean(x * x, axis=-1, keepdims=True) + EPS)


def _memkv_kernel(x_ref, g_ref, w_ref, o32_ref, o16_ref):
    x = x_ref[0]
    h = ((x * _rms_scale(x)) * g_ref[0]).astype(BF16)
    kv = jnp.dot(h, w_ref[0], preferred_element_type=F32)
    o32_ref[0, 0] = kv
    o16_ref[0, 0] = kv.astype(BF16)


def _memkv(mem_prompt, norm_mem, w_mem_kv_bf):
    bp = mem_prompt.shape[0]
    return pl.pallas_call(
        _memkv_kernel,
        grid=(DEPTH, bp),
        in_specs=[
            pl.BlockSpec((1, MEM_LEN, D_MODEL), lambda l, b: (b, 0, 0)),
            pl.BlockSpec((1, 1, D_MODEL), lambda l, b: (l, 0, 0)),
            pl.BlockSpec((1, D_MODEL, 2 * MEM_W), lambda l, b: (l, 0, 0)),
        ],
        out_specs=[
            pl.BlockSpec((1, 1, MEM_LEN, 2 * MEM_W), lambda l, b: (l, b, 0, 0)),
            pl.BlockSpec((1, 1, MEM_LEN, 2 * MEM_W), lambda l, b: (l, b, 0, 0)),
        ],
        out_shape=[
            jax.ShapeDtypeStruct((DEPTH, bp, MEM_LEN, 2 * MEM_W), F32),
            jax.ShapeDtypeStruct((DEPTH, bp, MEM_LEN, 2 * MEM_W), BF16),
        ],
        compiler_params=pltpu.CompilerParams(dimension_semantics=("arbitrary", "arbitrary")),
        name="memkv",
    )(mem_prompt, norm_mem.reshape(DEPTH, 1, D_MODEL), w_mem_kv_bf)


def _inproj_kernel(x0_ref, xn_ref, g_ref, w_ref, wdt_ref, cs_ref, cw_ref, cb_ref, dtb_ref,
                   q_ref, k_ref, v_ref, ag_ref, z_ref, xs_ref, bc_ref, dt_ref, mq_ref, mg_ref,
                   krow_ref, vrow_ref, crow_ref, ext_ref, h_scr, rz_ref, ra_ref, rg_ref, rm_ref, rq_ref,
                   rk_ref, rv_ref, *, tm, nt, keep_tiles):
    i = pl.program_id(1)
    step = pl.program_id(0) * nt + i
    slot = step % 2

    def norm_rows(src_ref, dst_slot):
        rb = min(tm, CONV_ROWS)
        for r0 in range(0, tm, rb):
            x = src_ref[0, r0:r0 + rb, :]
            h_scr[dst_slot, r0:r0 + rb, :] = ((x * _rms_scale(x)) * g_ref[...]).astype(BF16)

    @pl.when(step == 0)
    def _():
        norm_rows(x0_ref, 0)

    @pl.when(i == 0)
    def _():
        ext_ref[0:SUBLANES, :] = cs_ref[0]

    h = h_scr[slot]
    rb = min(tm, CONV_ROWS)

    def proj_into(dst_ref, r_off, lo, hi):
        dst_ref[r_off:r_off + tm, 0:hi - lo] = jnp.dot(h, w_ref[:, lo:hi], preferred_element_type=F32)

    def epilogue(src_ref, fn, out_ref, width):
        for r0 in range(0, tm, rb):
            for c0 in range(0, width, CONV_COLS):
                cols = slice(c0, c0 + CONV_COLS)
                out_ref[0, r0:r0 + rb, cols] = fn(src_ref[r0:r0 + rb, cols]).astype(BF16)

    proj_into(ext_ref, SUBLANES, C_XBC, C_MQ)
    proj_into(rz_ref, 0, C_Z, C_XBC)
    for r0 in range(0, tm, rb):
        for c0 in range(0, CONV_DIM, CONV_COLS):
            cols = slice(c0, c0 + CONV_COLS)
            y = cb_ref[:, cols]
            for tap in range(CONV_K):
                lo = SUBLANES - (CONV_K - 1) + tap + r0
                y = y + ext_ref[lo:lo + rb, cols] * cw_ref[tap:tap + 1, cols]
            xc = _silu(y).astype(BF16)
            if c0 < SSD_W:
                xs_ref[0, r0:r0 + rb, cols] = xc
            else:
                bc_ref[0, r0:r0 + rb, c0 - SSD_W:c0 - SSD_W + CONV_COLS] = xc
    tail = ext_ref[tm:tm + SUBLANES, :]
    crow_ref[0] = tail
    ext_ref[0:SUBLANES, :] = tail

    proj_into(ra_ref, 0, C_AG, C_Z)
    epilogue(rz_ref, _silu, z_ref, SSD_W)
    proj_into(rg_ref, 0, C_MG, C_END)
    norm_rows(xn_ref, 1 - slot)
    epilogue(ra_ref, _silu, ag_ref, ATT_W)
    dtr = jnp.dot(h, wdt_ref[...], preferred_element_type=F32) + dtb_ref[...]
    dt_ref[0] = jnp.maximum(dtr, 0.0) + jnp.log1p(jnp.exp(-jnp.abs(dtr)))
    proj_into(rm_ref, 0, C_MQ, C_MG)
    epilogue(rg_ref, _silu, mg_ref, MEM_W)
    proj_into(rq_ref, 0, C_Q, C_K)
    epilogue(rm_ref, lambda a: a * MEM_SCALE, mq_ref, MEM_W)
    proj_into(rk_ref, 0, C_K, C_V)
    epilogue(rq_ref, lambda a: a * ATT_SCALE, q_ref, ATT_W)
    proj_into(rv_ref, 0, C_V, C_AG)
    epilogue(rk_ref, lambda a: a, k_ref, ATT_W)
    epilogue(rv_ref, lambda a: a, v_ref, ATT_W)

    @pl.when(i >= nt - keep_tiles)
    def _():
        off = pl.multiple_of((i - (nt - keep_tiles)) * tm, tm)
        krow_ref[0, pl.ds(off, tm), :] = rk_ref[...]
        vrow_ref[0, pl.ds(off, tm), :] = rv_ref[...]


def _inproj(x, conv_state8, lw, *, tm, keep):
    bt, t, _ = x.shape
    nt = t // tm
    keep_tiles = keep // tm
    tile = lambda w: pl.BlockSpec((1, tm, w), lambda b, i: (b, i, 0))
    const2 = lambda a: pl.BlockSpec(a.shape, lambda b, i: (0, 0))
    bf = lambda w: jax.ShapeDtypeStruct((bt, t, w), BF16)
    bc_w = CONV_DIM - SSD_W
    kernel = functools.partial(_inproj_kernel, tm=tm, nt=nt, keep_tiles=keep_tiles)
    last = bt * nt - 1

    def next_tile(b, i):
        nxt = jnp.minimum(b * nt + i + 1, last)
        return (nxt // nt, nxt % nt, 0)

    return pl.pallas_call(
        kernel,
        grid=(bt, nt),
        in_specs=[
            pl.BlockSpec((1, tm, D_MODEL), lambda b, i: (0, 0, 0)),
            pl.BlockSpec((1, tm, D_MODEL), next_tile),
            const2(lw["g_pre"]), const2(lw["w_main"]), const2(lw["w_dt"]),
            pl.BlockSpec((1, SUBLANES, CONV_DIM), lambda b, i: (b, 0, 0)),
            const2(lw["conv_w"]), const2(lw["conv_b"]), const2(lw["dt_bias"]),
        ],
        out_specs=[
            tile(ATT_W), tile(ATT_W), tile(ATT_W), tile(ATT_W), tile(SSD_W), tile(SSD_W), tile(bc_w),
            tile(LANES), tile(MEM_W), tile(MEM_W),
            pl.BlockSpec((1, keep, ATT_W), lambda b, i: (b, 0, 0)),
            pl.BlockSpec((1, keep, ATT_W), lambda b, i: (b, 0, 0)),
            pl.BlockSpec((1, SUBLANES, CONV_DIM), lambda b, i: (b, 0, 0)),
        ],
        out_shape=[
            bf(ATT_W), bf(ATT_W), bf(ATT_W), bf(ATT_W), bf(SSD_W), bf(SSD_W), bf(bc_w),
            jax.ShapeDtypeStruct((bt, t, LANES), F32), bf(MEM_W), bf(MEM_W),
            jax.ShapeDtypeStruct((bt, keep, ATT_W), F32),
            jax.ShapeDtypeStruct((bt, keep, ATT_W), F32),
            jax.ShapeDtypeStruct((bt, SUBLANES, CONV_DIM), F32),
        ],
        scratch_shapes=([pltpu.VMEM((SUBLANES + tm, CONV_DIM), F32), pltpu.VMEM((2, tm, D_MODEL), BF16)]
                        + [pltpu.VMEM((tm, w), F32) for w in (SSD_W, ATT_W, MEM_W, MEM_W, ATT_W, ATT_W, ATT_W)]),
        compiler_params=pltpu.CompilerParams(
            dimension_semantics=("arbitrary", "arbitrary"), vmem_limit_bytes=VMEM_LIMIT),
        name="inproj",
    )(x, x, lw["g_pre"], lw["w_main"], lw["w_dt"], conv_state8, lw["conv_w"], lw["conv_b"], lw["dt_bias"])


SSD_L = LANES


def _ssd_kernel(xs_ref, bc_ref, dt_ref, z_ref, h0_ref, alog_ref, dsk_ref, gs_ref,
                o_ref, ht_ref, h_scr, *, nb, nsub, nsteps):
    i = pl.program_id(0)

    @pl.when(i == 0)
    def _():
        h_scr[...] = h0_ref[...]

    neg_a = -jnp.exp(alog_ref[...])
    ri = lax.broadcasted_iota(jnp.int32, (SSD_L, LANES), 0)
    ci = lax.broadcasted_iota(jnp.int32, (SSD_L, LANES), 1)
    causal = ci <= ri
    tril = jnp.where(causal, 1.0, 0.0).astype(F32)
    lo_half = ci < SSD_HEAD_DIM
    top_rows = ri < SSD_HEAD_DIM
    lane1 = lax.broadcasted_iota(jnp.int32, (1, LANES), 1)
    hm0 = jnp.where(lane1 < SSD_HEAD_DIM, 1.0, 0.0).astype(BF16)
    hm1 = jnp.where(lane1 < SSD_HEAD_DIM, 0.0, 1.0).astype(BF16)
    pairs_per_group = SSD_HEADS // SSD_GROUPS // 2
    half = SSD_W // SSD_GROUPS

    for b in range(nb):
        for c in range(nsub):
            rows = slice(c * SSD_L, (c + 1) * SSD_L)
            dtc = dt_ref[b, rows, :]
            acum = jnp.dot(tril, dtc * neg_a, precision=lax.Precision.HIGHEST,
                           preferred_element_type=F32)
            acum_t = acum.T
            dt_t = dtc.T
            e_last = jnp.exp(acum[SSD_L - 1:SSD_L, :])
            bcv = bc_ref[b, rows, :]
            y_blocks = []
            for g in range(SSD_GROUPS):
                bg = bcv[:, g * SSD_STATE:(g + 1) * SSD_STATE]
                cg = bcv[:, (SSD_GROUPS + g) * SSD_STATE:(SSD_GROUPS + g + 1) * SSD_STATE]
                cb = lax.dot_general(cg, bg, NT_DIMS, preferred_element_type=F32)
                for j in range(pairs_per_group):
                    pair = g * pairs_per_group + j
                    e0 = 2 * pair
                    blk = slice(pair * LANES, (pair + 1) * LANES)
                    a_cols = [jnp.broadcast_to(acum[:, e:e + 1], (SSD_L, LANES)) for e in (e0, e0 + 1)]
                    d_cols = [jnp.broadcast_to(dtc[:, e:e + 1], (SSD_L, LANES)) for e in (e0, e0 + 1)]
                    ws = []
                    for k in range(2):
                        decay = jnp.exp(jnp.where(causal, a_cols[k] - acum_t[e0 + k:e0 + k + 1, :], NEG_INF))
                        ws.append((cb * decay * dt_t[e0 + k:e0 + k + 1, :]).astype(BF16))
                    w2 = jnp.concatenate(ws, axis=1)
                    acum2 = jnp.where(lo_half, a_cols[0], a_cols[1])
                    dt2 = jnp.where(lo_half, d_cols[0], d_cols[1])
                    x2 = xs_ref[b, rows, blk]
                    xbd = jnp.concatenate([x2 * hm0, x2 * hm1], axis=0)
                    hp = h_scr[b, pair]
                    y2 = jnp.dot(w2, xbd, preferred_element_type=F32)
                    y2 = y2 + lax.dot_general(cg, hp.astype(BF16), NT_DIMS,
                                              preferred_element_type=F32) * jnp.exp(acum2)
                    y_blocks.append(y2)
                    to_end = jnp.exp(acum2[SSD_L - 1:SSD_L, :] - acum2) * dt2
                    xe2 = (x2.astype(F32) * to_end).astype(BF16)
                    hn = lax.dot_general(xe2, bg, TN_DIMS, preferred_element_type=F32)
                    scale = jnp.where(top_rows, e_last[:, e0:e0 + 1], e_last[:, e0 + 1:e0 + 2])
                    h_scr[b, pair] = hp * scale + hn

            y = jnp.concatenate(y_blocks, axis=1) + xs_ref[b, rows, :].astype(F32) * dsk_ref[...]
            y = y * z_ref[b, rows, :].astype(F32)
            for g in range(SSD_GROUPS):
                yg = y[:, g * half:(g + 1) * half]
                o_ref[b, rows, g * half:(g + 1) * half] = (
                    yg * _rms_scale(yg) * gs_ref[:, g * half:(g + 1) * half]).astype(BF16)

    @pl.when(i == nsteps - 1)
    def _():
        ht_ref[...] = h_scr[...]


def _ssd(xs, bc, dt, zs, h0, lw, *, nsub):
    bt, t, _ = xs.shape
    rows = nsub * SSD_L
    nsteps = t // rows
    tile = lambda w: pl.BlockSpec((bt, rows, w), lambda i: (0, i, 0))
    const2 = lambda a: pl.BlockSpec(a.shape, lambda i: (0, 0))
    npair = SSD_HEADS // 2
    state_shape = (bt, npair, 2 * SSD_HEAD_DIM, SSD_STATE)
    state_spec = pl.BlockSpec(state_shape, lambda i: (0, 0, 0, 0))
    kernel = functools.partial(_ssd_kernel, nb=bt, nsub=nsub, nsteps=nsteps)
    return pl.pallas_call(
        kernel,
        grid=(nsteps,),
        in_specs=[tile(SSD_W), tile(CONV_DIM - SSD_W), tile(LANES), tile(SSD_W), state_spec,
                  const2(lw["a_log"]), const2(lw["d_skip"]), const2(lw["g_ssd"])],
        out_specs=[tile(SSD_W), state_spec],
        out_shape=[jax.ShapeDtypeStruct((bt, t, SSD_W), BF16), jax.ShapeDtypeStruct(state_shape, F32)],
        scratch_shapes=[pltpu.VMEM(state_shape, F32)],
        compiler_params=pltpu.CompilerParams(
            dimension_semantics=("arbitrary",), vmem_limit_bytes=VMEM_LIMIT),
        name="ssd",
    )(xs, bc, dt, zs, h0, lw["a_log"], lw["d_skip"], lw["g_ssd"])


def _attn_out_kernel(*refs, tq, n_prev, mask_start):
    q_ref = refs[0]
    k_refs = refs[1:2 + n_prev]
    v_refs = refs[2 + n_prev:3 + 2 * n_prev]
    (ag_ref, s_ref, mq_ref, mg_ref, mk_ref, mv_ref, x_ref, bias_ref, wo_ref, gpost_ref,
     o_ref, a_scr, m_scr, s_scr, p_scr, mix_scr) = refs[3 + 2 * n_prev:19 + 2 * n_prev]
    if mask_start:
        bias_scr = refs[19 + 2 * n_prev]
    i = pl.program_id(1)

    pieces = []
    off = 0
    for kr, vr in zip(k_refs, v_refs):
        n = kr.shape[-2]
        pieces.append((kr, vr, (0,) * (len(kr.shape) - 2), off, n))
        off += n

    lane1 = lax.broadcasted_iota(jnp.int32, (1, LANES), 1)
    hm = (jnp.where(lane1 < ATT_HEAD_DIM, 1.0, 0.0).astype(BF16),
          jnp.where(lane1 < ATT_HEAD_DIM, 0.0, 1.0).astype(BF16))
    lo_half = lax.broadcasted_iota(jnp.int32, (tq, LANES), 1) < ATT_HEAD_DIM

    if mask_start:
        @pl.when(i <= n_prev)
        def _():
            widx = lax.broadcasted_iota(jnp.int32, (1, ATT_PAST + tq), 1)
            penalty = jnp.where(widx + (i * tq - ATT_PAST) >= 0, 0.0, NEG_INF)
            for hd in range(ATT_HEADS):
                bias_scr[hd] = bias_ref[hd] + penalty
        bias_src = bias_scr
    else:
        bias_src = bias_ref

    for j in range(ATT_W // LANES):
        blk = slice(j * LANES, (j + 1) * LANES)
        q2 = q_ref[0, :, blk]
        outs = []
        for h2 in range(2):
            qm = q2 * hm[h2]
            for kr, _, lead, o, n in pieces:
                s_scr[h2, :, o:o + n] = lax.dot_general(qm, kr[lead + (slice(None), blk)], NT_DIMS,
                                                        preferred_element_type=F32)
            dens = []
            for r0 in range(0, tq, SOFTMAX_ROWS):
                rws = slice(r0, r0 + SOFTMAX_ROWS)
                sb = s_scr[h2, rws, :] + bias_src[2 * j + h2, rws, :]
                p = jnp.exp(sb - jnp.max(sb, axis=-1, keepdims=True))
                dens.append(jnp.sum(p, axis=-1, keepdims=True))
                p_scr[h2, rws, :] = p.astype(BF16)
            den = jnp.concatenate(dens, axis=0)
            pv = functools.reduce(jnp.add, [
                jnp.dot(p_scr[h2, :, o:o + n], vr[lead + (slice(None), blk)], preferred_element_type=F32)
                for _, vr, lead, o, n in pieces])
            outs.append(pv * (1.0 / den))
        o2 = jnp.where(lo_half, outs[0], outs[1])
        a_scr[:, blk] = (o2 * ag_ref[0, :, blk].astype(F32)).astype(BF16)

    for hh in range(MEM_HEADS):
        blk = slice(hh * MEM_HEAD_DIM, (hh + 1) * MEM_HEAD_DIM)
        slot = hh % 2
        s_scr[slot, :, 0:MEM_LEN] = lax.dot_general(mq_ref[0, :, blk], mk_ref[0, 0, :, blk], NT_DIMS,
                                                    preferred_element_type=F32)
        dens = []
        for r0 in range(0, tq, SOFTMAX_ROWS):
            rws = slice(r0, r0 + SOFTMAX_ROWS)
            sb = s_scr[slot, rws, 0:MEM_LEN]
            p = jnp.exp(sb - jnp.max(sb, axis=-1, keepdims=True))
            dens.append(jnp.sum(p, axis=-1, keepdims=True))
            p_scr[slot, rws, 0:MEM_LEN] = p.astype(BF16)
        den = jnp.concatenate(dens, axis=0)
        o = jnp.dot(p_scr[slot, :, 0:MEM_LEN], mv_ref[0, 0, :, blk], preferred_element_type=F32) * (1.0 / den)
        m_scr[:, blk] = (o * mg_ref[0, :, blk].astype(F32)).astype(BF16)

    mix = jnp.dot(a_scr[...], wo_ref[0:ATT_W, :], preferred_element_type=F32)
    mix = mix + jnp.dot(s_ref[0], wo_ref[ATT_W:ATT_W + SSD_W, :], preferred_element_type=F32)
    mix_scr[...] = mix + jnp.dot(m_scr[...], wo_ref[ATT_W + SSD_W:D_MIX, :], preferred_element_type=F32)
    for r0 in range(0, tq, SOFTMAX_ROWS):
        rws = slice(r0, r0 + SOFTMAX_ROWS)
        mb = mix_scr[rws, :]
        o_ref[0, rws, :] = x_ref[0, rws, :] + mb * _rms_scale(mb) * gpost_ref[...]


def _attn_out(x, q, k_list, v_list, ag, s_out, mq, mg, mk, mv, bias, lw, *, tq, mask_start):
    bt, t, _ = x.shape
    n_prev = len(k_list) - 1
    tile = lambda w: pl.BlockSpec((1, tq, w), lambda b, i: (b, i, 0))
    const2 = lambda a: pl.BlockSpec(a.shape, lambda b, i: (0, 0))
    const3 = lambda a: pl.BlockSpec(a.shape, lambda b, i: (0, 0, 0))
    win_spec = lambda piece: pl.BlockSpec(piece[1], piece[2])
    mem_spec = lambda m: pl.BlockSpec((1, 1, MEM_LEN, MEM_W), m[1])
    kernel = functools.partial(_attn_out_kernel, tq=tq, n_prev=n_prev, mask_start=mask_start)
    return pl.pallas_call(
        kernel,
        grid=(bt, t // tq),
        in_specs=([tile(ATT_W)] + [win_spec(p) for p in k_list] + [win_spec(p) for p in v_list]
                  + [tile(ATT_W), tile(SSD_W), tile(MEM_W), tile(MEM_W), mem_spec(mk), mem_spec(mv),
                     tile(D_MODEL), const3(bias), const2(lw["w_out"]), const2(lw["g_post"])]),
        out_specs=tile(D_MODEL),
        out_shape=jax.ShapeDtypeStruct((bt, t, D_MODEL), F32),
        scratch_shapes=([pltpu.VMEM((tq, ATT_W), BF16), pltpu.VMEM((tq, MEM_W), BF16),
                         pltpu.VMEM((2, tq, ATT_PAST + tq), F32), pltpu.VMEM((2, tq, ATT_PAST + tq), BF16),
                         pltpu.VMEM((tq, D_MODEL), F32)]
                        + ([pltpu.VMEM(bias.shape, F32)] if mask_start else [])),
        compiler_params=pltpu.CompilerParams(
            dimension_semantics=("arbitrary", "arbitrary"), vmem_limit_bytes=VMEM_LIMIT),
        name="attn_out",
    )(q, *[p[0] for p in k_list], *[p[0] for p in v_list], ag, s_out, mq, mg, mk[0], mv[0], x,
      bias, lw["w_out"], lw["g_post"])


def _trunk_layer(x, mk, mv, conv_state8, h0, kv_cache, bias, lw, *, tm, tq, nsub, keep):
    (q, k, v, ag, zs, xs, bc, dt, mq, mg, krow, vrow, crow) = _inproj(x, conv_state8, lw, tm=tm, keep=keep)
    t = x.shape[1]
    pad_t = (-t) % SSD_L
    pad_rows = lambda a: jnp.pad(a, ((0, 0), (0, pad_t), (0, 0))) if pad_t else a
    s_out, h_t = _ssd(pad_rows(xs), pad_rows(bc), pad_rows(dt), pad_rows(zs), h0, lw, nsub=nsub)
    s_out = s_out[:, :t] if pad_t else s_out
    blk = (1, tq, ATT_W)
    if kv_cache is None:
        nprev = ATT_PAST // tq
        prev = [(lambda b, i, d=d: (b, jnp.maximum(i - d, 0), 0)) for d in range(nprev, 0, -1)]
        cur = lambda b, i: (b, i, 0)
        k_list = [(k, blk, m) for m in prev] + [(k, blk, cur)]
        v_list = [(v, blk, m) for m in prev] + [(v, blk, cur)]
    else:
        whole = lambda b, i: (b, 0, 0)
        k_list = [kv_cache[0], (k, blk, whole)]
        v_list = [kv_cache[1], (v, blk, whole)]
    y = _attn_out(x, q, k_list, v_list, ag, s_out, mq, mg, mk, mv, bias, lw, tq=tq,
                  mask_start=kv_cache is None)
    return y, krow, vrow, crow, h_t


def _rel_bias_table(rel_bias, tq):
    ncols = ATT_PAST + tq
    span = CHUNK + BAND - 1
    u = jnp.arange(span)
    dist = ATT_PAST - jnp.where(u < BAND, u, u - span)
    per_dist = rel_bias[:, :, jnp.clip(dist, -REL_CLIP, REL_CLIP) + REL_CLIP].astype(F32)
    skew = jnp.tile(per_dist, (1, 1, CHUNK))[:, :, :CHUNK * (span - 1)]
    band = skew.reshape(DEPTH, ATT_HEADS, CHUNK, span - 1)[:, :, :, :BAND]
    chunks = [jnp.pad(band, ((0, 0), (0, 0), (0, 0), (a * CHUNK, ncols - BAND - a * CHUNK)),
                      constant_values=NEG_INF) for a in range(tq // CHUNK)]
    return jnp.concatenate(chunks, axis=2)


def kernel(x_prompt, x_sample, cache_att_k, cache_att_v, state_conv, state_ssd, cache_mem_k, cache_mem_v,
           mem_prompt, w_in, w_out, w_mem_kv, norm_pre, norm_post, norm_mem, rel_bias, conv_w, conv_b,
           dt_bias, a_log, d_skip, norm_ssd):
    bp, seq, _ = x_prompt.shape
    bs, tdec, _ = x_sample.shape
    assert tdec == CHUNK and cache_att_k.shape[2] == ATT_PAST and seq % 256 == 0

    dt_lo = 4 * ATT_W + SSD_W + CONV_DIM
    w_main = jnp.concatenate([w_in[:, :, :dt_lo], w_in[:, :, dt_lo + SSD_HEADS:]], axis=-1).astype(BF16)
    w_dt = jnp.pad(w_in[:, :, dt_lo:dt_lo + SSD_HEADS], ((0, 0), (0, 0), (0, LANES - SSD_HEADS))).astype(BF16)
    w_out_bf = w_out.astype(BF16)
    pad_heads = lambda a: jnp.pad(a, ((0, 0), (0, LANES - SSD_HEADS)))[:, None, :]
    dt_bias_p = pad_heads(dt_bias)
    a_log_p = pad_heads(a_log)
    d_skip_x = jnp.repeat(d_skip, SSD_HEAD_DIM, axis=-1)[:, None, :]
    tq_prompt = 256
    bias_prompt = _rel_bias_table(rel_bias, tq_prompt)
    bias_sample = _rel_bias_table(rel_bias, CHUNK)

    mkv32, mkv16 = _memkv(mem_prompt, norm_mem, w_mem_kv.astype(BF16))
    cmem_k = cache_mem_k.reshape(DEPTH, bs, MEM_LEN, MEM_W).astype(BF16)
    cmem_v = cache_mem_v.reshape(DEPTH, bs, MEM_LEN, MEM_W).astype(BF16)
    catt_k = cache_att_k.reshape(DEPTH, bs, ATT_PAST, ATT_W).astype(BF16)
    catt_v = cache_att_v.reshape(DEPTH, bs, ATT_PAST, ATT_W).astype(BF16)

    pad_state = lambda s: jnp.pad(s, ((0, 0), (SUBLANES - (CONV_K - 1), 0), (0, 0)))
    npair = SSD_HEADS // 2
    conv0 = jnp.zeros((bp, SUBLANES, CONV_DIM), F32)
    ssd0 = jnp.zeros((bp, npair, 2 * SSD_HEAD_DIM, SSD_STATE), F32)

    hp, hs = x_prompt, x_sample
    outs = {n: [] for n in ("pk", "pv", "pc", "ps", "sk", "sv", "sc", "ss")}
    for l in range(DEPTH):
        lw = dict(g_pre=norm_pre[l][None, :], w_main=w_main[l], w_dt=w_dt[l], conv_w=conv_w[l],
                  conv_b=conv_b[l][None, :], dt_bias=dt_bias_p[l], a_log=a_log_p[l], d_skip=d_skip_x[l],
                  g_ssd=norm_ssd[l][None, :], w_out=w_out_bf[l], g_post=norm_post[l][None, :])
        layer_blk = lambda b, i, l=l: (l, b, 0, 0)
        hp, ka, va, ca, sa = _trunk_layer(
            hp, (mkv16, layer_blk), (mkv16, lambda b, i, l=l: (l, b, 0, 1)), conv0, ssd0, None,
            bias_prompt[l], lw, tm=256, tq=tq_prompt, nsub=2, keep=ATT_PAST)
        outs["pk"].append(ka); outs["pv"].append(va); outs["pc"].append(ca); outs["ps"].append(sa)
        cache_blk = (1, 1, ATT_PAST, ATT_W)
        hs, kb, vb, cb, sb = _trunk_layer(
            hs, (cmem_k, layer_blk), (cmem_v, layer_blk),
            pad_state(state_conv[l]), state_ssd[l].reshape(bs, npair, 2 * SSD_HEAD_DIM, SSD_STATE),
            ((catt_k, cache_blk, layer_blk), (catt_v, cache_blk, layer_blk)),
            bias_sample[l], lw, tm=CHUNK, tq=CHUNK, nsub=1, keep=CHUNK)
        outs["sk"].append(kb); outs["sv"].append(vb); outs["sc"].append(cb); outs["ss"].append(sb)

    heads = lambda rows, b: jnp.stack(rows).reshape(DEPTH, b, -1, ATT_HEADS, ATT_HEAD_DIM)
    conv_rows = lambda rows: jnp.stack(rows)[:, :, SUBLANES - (CONV_K - 1):, :]
    ssd_state = lambda rows, b: jnp.stack(rows).reshape(DEPTH, b, SSD_HEADS, SSD_HEAD_DIM, SSD_STATE)
    p_mem_k = mkv32[..., :MEM_W].reshape(DEPTH, bp, MEM_LEN, MEM_HEADS, MEM_HEAD_DIM)
    p_mem_v = mkv32[..., MEM_W:].reshape(DEPTH, bp, MEM_LEN, MEM_HEADS, MEM_HEAD_DIM)
    return (hp, hs, heads(outs["pk"], bp), heads(outs["pv"], bp), conv_rows(outs["pc"]),
            ssd_state(outs["ps"], bp), p_mem_k, p_mem_v, heads(outs["sk"], bs), heads(outs["sv"], bs),
            conv_rows(outs["sc"]), ssd_state(outs["ss"], bs))
```
